```python
import math
import jax, jax.numpy as jnp
from jax import lax
import numpy as np

D_MODEL = 1024
BATCH = 16
SEQ = 4096
DEPTH = 1
DEC_BATCH = 2
DEC_SEQ = 16384
PAST_LEN = 128

GRID_W = 64
HEAD_DIM = 64
RWKV_WIDTH = 512
RWKV_HEADS = RWKV_WIDTH // HEAD_DIM
ATTN_WIDTH = D_MODEL - RWKV_WIDTH
ATTN_Q_HEADS = ATTN_WIDTH // HEAD_DIM
ATTN_KV_HEADS = 2
KV_WIDTH = ATTN_KV_HEADS * HEAD_DIM
W_LORA = 64
A_LORA = 64
G_LORA = 128
RWKV_COLS = 3 * RWKV_WIDTH + W_LORA + A_LORA + G_LORA
IN_COLS = RWKV_COLS + ATTN_WIDTH + 2 * KV_WIDTH
Q_BLOCK = 128
ROPE_THETA = 10000.0
ROPE_PAIRS = HEAD_DIM // 4
N_EXPERTS = 16
CAPACITY_FACTOR = 2
D_EXPERT = 1024
NORM_EPS = 1e-6
GN_EPS = 64e-5

kernel_name = "hymba_rwkv7_gqa_axial_ec_moe_encoder"


def rms_norm(x, gain, eps=NORM_EPS):
    xf = x.astype(jnp.float32)
    y = xf * lax.rsqrt(jnp.mean(xf * xf, axis=-1, keepdims=True) + eps)
    return (y * gain.astype(jnp.float32)).astype(x.dtype)


def axial_angles(n):
    rows = n // GRID_W
    row = jnp.repeat(jnp.arange(rows, dtype=jnp.float32), GRID_W)
    col = jnp.tile(jnp.arange(GRID_W, dtype=jnp.float32), rows)
    inv_freq = ROPE_THETA ** (-jnp.arange(ROPE_PAIRS, dtype=jnp.float32) / ROPE_PAIRS)
    return row[:, None] * inv_freq, col[:, None] * inv_freq


def rope_half(x, ang):
    cos = jnp.cos(ang)[None, :, None, :]
    sin = jnp.sin(ang)[None, :, None, :]
    x1, x2 = x[..., :ROPE_PAIRS], x[..., ROPE_PAIRS:]
    return jnp.concatenate([x1 * cos - x2 * sin, x2 * cos + x1 * sin], axis=-1)


def axial_rope(x, ang_row, ang_col):
    xf = x.astype(jnp.float32)
    half = HEAD_DIM // 2
    out = jnp.concatenate([rope_half(xf[..., :half], ang_row), rope_half(xf[..., half:], ang_col)], axis=-1)
    return out.astype(x.dtype)


def block_attention(q, k, v):
    b, n, hq, dh = q.shape
    hkv = k.shape[2]
    grp = hq // hkv
    nblk = n // Q_BLOCK
    qb = q.reshape(b, nblk, Q_BLOCK, hkv, grp, dh).transpose(1, 0, 2, 3, 4, 5)
    scale = 1.0 / math.sqrt(dh)

    def one_block(qi):
        s = jnp.einsum('bqkgd,bskd->bkgqs', qi, k).astype(jnp.float32) * scale
        p = jax.nn.softmax(s, axis=-1).astype(v.dtype)
        return jnp.einsum('bkgqs,bskd->bqkgd', p, v)

    out = lax.map(one_block, qb)
    return out.transpose(1, 0, 2, 3, 4, 5).reshape(b, n, hq * dh)


def centred_shift(x, taps):
    prev = jnp.pad(x[:, :-1], ((0, 0), (1, 0), (0, 0)))
    nxt = jnp.pad(x[:, 1:], ((0, 0), (0, 1), (0, 0)))
    return taps[0] * prev + taps[1] * x + taps[2] * nxt


def wkv7_scan(r, decay, k, v, kk_neg, kk_b, reverse):
    b, n, h, d = r.shape

    def step(state, inp):
        r_t, w_t, k_t, v_t, an_t, b_t = inp
        sa = jnp.einsum('bhij,bhj->bhi', state, an_t)
        state = (state * w_t[:, :, None, :] + sa[..., None] * b_t[:, :, None, :]
                 + v_t[..., None] * k_t[:, :, None, :])
        return state, jnp.einsum('bhij,bhj->bhi', state, r_t)

    s0 = jnp.zeros((b, h, d, d), jnp.float32)
    xs = tuple(jnp.swapaxes(t, 0, 1) for t in (r, decay, k, v, kk_neg, kk_b))
    _, ys = lax.scan(step, s0, xs, reverse=reverse)
    return jnp.swapaxes(ys, 0, 1)


def rwkv7_mix(feat, w0, w_lora_up, a0, a_lora_up, k_k, k_a, r_k, g_lora_up, ln_g, ln_b):
    f32 = jnp.float32
    b, n, _ = feat.shape
    feat = feat.astype(f32)
    splits = [RWKV_WIDTH, 2 * RWKV_WIDTH, 3 * RWKV_WIDTH,
              3 * RWKV_WIDTH + W_LORA, 3 * RWKV_WIDTH + W_LORA + A_LORA]
    r, k, v, wd, ad, gd = jnp.split(feat, splits, axis=-1)
    heads = lambda t: t.reshape(b, n, RWKV_HEADS, HEAD_DIM)
    kk = heads(k * k_k.astype(f32))
    kk = kk / jnp.maximum(jnp.linalg.norm(kk, axis=-1, keepdims=True), 1e-12)
    gate = jax.nn.sigmoid(gd) @ g_lora_up.astype(f32)
    tw = jnp.tanh(wd)
    y = jnp.zeros((b, n, RWKV_HEADS, HEAD_DIM), f32)
    k_bonus = jnp.zeros_like(k)
    for direction, reverse in ((0, False), (1, True)):
        w_log = -jax.nn.softplus(-(w0[direction].astype(f32) + tw @ w_lora_up[direction].astype(f32))) - 0.5
        decay = jnp.exp(-jnp.exp(w_log))
        a = jax.nn.sigmoid(a0[direction].astype(f32) + ad @ a_lora_up[direction].astype(f32))
        k_dir = k * (1.0 + (a - 1.0) * k_a.astype(f32))
        y = y + wkv7_scan(heads(r), heads(decay), heads(k_dir), heads(v), -kk, kk * heads(a), reverse)
        k_bonus = k_bonus + 0.5 * k_dir
    mu = jnp.mean(y, axis=-1, keepdims=True)
    var = jnp.mean(jnp.square(y - mu), axis=-1, keepdims=True)
    y = ((y - mu) * lax.rsqrt(var + GN_EPS)).reshape(b, n, RWKV_WIDTH) * ln_g.astype(f32) + ln_b.astype(f32)
    bonus = jnp.sum(heads(r) * heads(k_bonus) * r_k.astype(f32), axis=-1, keepdims=True) * heads(v)
    return (y + bonus.reshape(b, n, RWKV_WIDTH)) * gate


def expert_choice_moe(x, w_router, w_gate, w_up, w_down):
    b, n, d = x.shape
    num = b * n
    tokens = x.reshape(num, d)
    cap = CAPACITY_FACTOR * num // N_EXPERTS
    aff = jax.nn.softmax((tokens @ w_router).astype(jnp.float32), axis=-1)
    gates, idx = lax.top_k(aff.T, cap)
    xs = tokens[idx]
    hid = jax.nn.silu(jnp.einsum('ecd,edf->ecf', xs, w_gate)) * jnp.einsum('ecd,edf->ecf', xs, w_up)
    out = jnp.einsum('ecf,efd->ecd', hid, w_down) * gates[..., None].astype(x.dtype)
    y = jnp.zeros_like(tokens).at[idx.reshape(-1)].add(out.reshape(-1, d))
    return y.reshape(b, n, d)


def encoder_layer(x, norm1, w_in, shift_taps, w0, w_lora_up, a0, a_lora_up, k_k, k_a, r_k,
                  g_lora_up, ln_g, ln_b, q_norm, k_norm, w_out, norm2, w_router,
                  expert_w_gate, expert_w_up, expert_w_down):
    b, n, _ = x.shape
    proj = rms_norm(x, norm1) @ w_in
    rwkv_feat = centred_shift(proj[..., :RWKV_COLS], shift_taps)
    q, k, v = jnp.split(proj[..., RWKV_COLS:], [ATTN_WIDTH, ATTN_WIDTH + KV_WIDTH], axis=-1)
    ang_row, ang_col = axial_angles(n)
    q = axial_rope(rms_norm(q.reshape(b, n, ATTN_Q_HEADS, HEAD_DIM), q_norm), ang_row, ang_col)
    k = axial_rope(rms_norm(k.reshape(b, n, ATTN_KV_HEADS, HEAD_DIM), k_norm), ang_row, ang_col)
    v = v.reshape(b, n, ATTN_KV_HEADS, HEAD_DIM)
    attn_out = block_attention(q, k, v)
    rwkv_out = rwkv7_mix(rwkv_feat, w0, w_lora_up, a0, a_lora_up, k_k, k_a, r_k,
                         g_lora_up, ln_g, ln_b).astype(x.dtype)
    h = x + jnp.concatenate([rwkv_out, attn_out], axis=-1) @ w_out
    return h + expert_choice_moe(rms_norm(h, norm2), w_router, expert_w_gate, expert_w_up, expert_w_down)


def run_trunk(x, layer_params):
    for l in range(DEPTH):
        x = encoder_layer(x, *(p[l] for p in layer_params))
    return x


def setup_inputs(seed: int = 0) -> dict:
    key = jax.random.key(seed)
    ks = jax.random.split(key, 24)
    nrm = lambda i, shape: jax.random.normal(ks[i], shape, jnp.float32)
    base_taps = jnp.array([0.25, 0.5, 0.25], jnp.float32)[None, :, None]
    return {
        "x_prompt": nrm(0, (BATCH, SEQ, D_MODEL)),
        "x_sample": nrm(1, (DEC_BATCH, DEC_SEQ, D_MODEL)),
        "norm1": 1.0 + 0.1 * nrm(2, (DEPTH, D_MODEL)),
        "w_in": nrm(3, (DEPTH, D_MODEL, IN_COLS)) * D_MODEL ** -0.5,
        "shift_taps": base_taps + 0.05 * nrm(4, (DEPTH, 3, RWKV_COLS)),
        "w0": 0.5 * nrm(5, (DEPTH, 2, RWKV_WIDTH)),
        "w_lora_up": nrm(6, (DEPTH, 2, W_LORA, RWKV_WIDTH)) * 0.5 * W_LORA ** -0.5,
        "a0": 0.5 * nrm(7, (DEPTH, 2, RWKV_WIDTH)),
        "a_lora_up": nrm(8, (DEPTH, 2, A_LORA, RWKV_WIDTH)) * 0.5 * A_LORA ** -0.5,
        "k_k": 1.0 + 0.1 * nrm(9, (DEPTH, RWKV_WIDTH)),
        "k_a": 1.0 + 0.1 * nrm(10, (DEPTH, RWKV_WIDTH)),
        "r_k": 0.1 * nrm(11, (DEPTH, RWKV_HEADS, HEAD_DIM)),
        "g_lora_up": nrm(12, (DEPTH, G_LORA, RWKV_WIDTH)) * G_LORA ** -0.5,
        "ln_g": 1.0 + 0.1 * nrm(13, (DEPTH, RWKV_WIDTH)),
        "ln_b": 0.02 * nrm(14, (DEPTH, RWKV_WIDTH)),
        "q_norm": 1.0 + 0.1 * nrm(15, (DEPTH, HEAD_DIM)),
        "k_norm": 1.0 + 0.1 * nrm(16, (DEPTH, HEAD_DIM)),
        "w_out": nrm(17, (DEPTH, D_MODEL, D_MODEL)) * D_MODEL ** -0.5,
        "norm2": 1.0 + 0.1 * nrm(18, (DEPTH, D_MODEL)),
        "w_router": nrm(19, (DEPTH, D_MODEL, N_EXPERTS)) * D_MODEL ** -0.5,
        "expert_w_gate": nrm(20, (DEPTH, N_EXPERTS, D_MODEL, D_EXPERT)) * D_MODEL ** -0.5,
        "expert_w_up": nrm(21, (DEPTH, N_EXPERTS, D_MODEL, D_EXPERT)) * D_MODEL ** -0.5,
        "expert_w_down": nrm(22, (DEPTH, N_EXPERTS, D_EXPERT, D_MODEL)) * D_EXPERT ** -0.5,
    }


def reference(x_prompt, x_sample, norm1, w_in, shift_taps, w0, w_lora_up, a0, a_lora_up,
              k_k, k_a, r_k, g_lora_up, ln_g, ln_b, q_norm, k_norm, w_out, norm2, w_router,
              expert_w_gate, expert_w_up, expert_w_down):
    layer_params = (norm1, w_in, shift_taps, w0, w_lora_up, a0, a_lora_up, k_k, k_a, r_k,
                    g_lora_up, ln_g, ln_b, q_norm, k_norm, w_out, norm2, w_router,
                    expert_w_gate, expert_w_up, expert_w_down)
    y_prompt = run_trunk(x_prompt, layer_params)
    y_sample = run_trunk(x_sample, layer_params)
    return (y_prompt, y_sample)
```

```python
import functools
import math

import jax
import jax.numpy as jnp
from jax import lax
from jax.experimental import pallas as pl
from jax.experimental.pallas import tpu as pltpu

F32 = jnp.float32
BF16 = jnp.bfloat16
I32 = jnp.int32

D_MODEL = 1024
HEAD_DIM = 64
RWKV_WIDTH = 512
RWKV_HEADS = RWKV_WIDTH // HEAD_DIM
ATTN_WIDTH = D_MODEL - RWKV_WIDTH
ATTN_Q_HEADS = ATTN_WIDTH // HEAD_DIM
ATTN_KV_HEADS = 2
KV_GROUP = ATTN_Q_HEADS // ATTN_KV_HEADS
KV_WIDTH = ATTN_KV_HEADS * HEAD_DIM
W_LORA = 64
A_LORA = 64
G_LORA = 128
RWKV_COLS = 3 * RWKV_WIDTH + W_LORA + A_LORA + G_LORA
IN_COLS = RWKV_COLS + ATTN_WIDTH + 2 * KV_WIDTH
QK_COLS = ATTN_WIDTH + KV_WIDTH
GRID_W = 64
ROPE_THETA = 10000.0
ROPE_PAIRS = HEAD_DIM // 4
N_EXPERTS = 16
CAPACITY_FACTOR = 2
D_EXPERT = 1024
NORM_EPS = 1e-6
GN_EPS = 64e-5
KK_EPS_SQ = 1e-24
DECAY_SCALE = math.exp(-0.5)

LANES = 128
CHUNK = 64
PAIR = 2 * HEAD_DIM
N_PAIRS = RWKV_WIDTH // PAIR
VMEM_LIMIT = 48 * 1024 * 1024


def _cparams(sem):
    return pltpu.CompilerParams(dimension_semantics=sem, vmem_limit_bytes=VMEM_LIMIT)


def _full(shape):
    nd = len(shape)
    return pl.BlockSpec(shape, lambda *_: (0,) * nd)


def _dot(a, b):
    return jnp.dot(a, b, preferred_element_type=F32)


def _dot_nt(a, b):
    return lax.dot_general(a, b, (((1,), (1,)), ((), ())), preferred_element_type=F32)


def _split2(x):
    hi = x.astype(BF16)
    lo = (x - hi.astype(F32)).astype(BF16)
    return hi, lo


def _split3(x):
    hi = x.astype(BF16)
    r1 = x - hi.astype(F32)
    mid = r1.astype(BF16)
    lo = (r1 - mid.astype(F32)).astype(BF16)
    return hi, mid, lo


def _seg_sum(x, seg):
    hi, lo = _split2(x)
    return _dot(hi, seg) + _dot(lo, seg)


def _sigmoid(x):
    return 1.0 / (1.0 + jnp.exp(-x))


def _in_proj_kernel(x_ref, g_ref, w_ref, qkg_ref, cos_ref, sin_ref, seg_ref,
                    feat_ref, q_ref, k_ref, v_ref):
    x = x_ref[...]
    ms = jnp.mean(x * x, axis=-1, keepdims=True)
    xn = (x * lax.rsqrt(ms + NORM_EPS) * g_ref[...]).astype(BF16)
    p = _dot(xn, w_ref[...])
    feat_ref[...] = p[:, :RWKV_COLS]
    qk = p[:, RWKV_COLS:RWKV_COLS + QK_COLS]
    ssq = _seg_sum(qk * qk, seg_ref[...])
    qkn = qk * lax.rsqrt(ssq * (1.0 / HEAD_DIM) + NORM_EPS) * qkg_ref[...]
    reps = QK_COLS // LANES
    cos = jnp.concatenate([cos_ref[...]] * reps, axis=1)
    sin = jnp.concatenate([sin_ref[...]] * reps, axis=1)
    lane = lax.broadcasted_iota(I32, qkn.shape, 1)
    first = (lane & (2 * ROPE_PAIRS - 1)) < ROPE_PAIRS
    partner = jnp.where(first, pltpu.roll(qkn, QK_COLS - ROPE_PAIRS, 1), pltpu.roll(qkn, ROPE_PAIRS, 1))
    rot = qkn * cos + partner * sin
    q_ref[...] = (rot[:, :ATTN_WIDTH] * (1.0 / math.sqrt(HEAD_DIM))).astype(BF16)
    kr = rot[:, ATTN_WIDTH:].astype(BF16)
    vv = p[:, RWKV_COLS + QK_COLS:].astype(BF16)
    for j in range(ATTN_KV_HEADS):
        k_ref[j] = kr[:, j * HEAD_DIM:(j + 1) * HEAD_DIM]
        v_ref[j] = vv[:, j * HEAD_DIM:(j + 1) * HEAD_DIM]


def _in_proj(x2, norm1, w_in, qk_gain, cos_t, sin_t, seg_qk, n, tm):
    t = x2.shape[0]
    nblk = n // tm
    return pl.pallas_call(
        _in_proj_kernel,
        grid=(t // tm,),
        in_specs=[
            pl.BlockSpec((tm, D_MODEL), lambda i: (i, 0)),
            _full((1, D_MODEL)),
            _full((D_MODEL, IN_COLS)),
            _full((1, QK_COLS)),
            pl.BlockSpec((tm, LANES), lambda i: (i % nblk, 0)),
            pl.BlockSpec((tm, LANES), lambda i: (i % nblk, 0)),
            _full((QK_COLS, QK_COLS)),
        ],
        out_specs=[
            pl.BlockSpec((tm, RWKV_COLS), lambda i: (i, 0)),
            pl.BlockSpec((tm, ATTN_WIDTH), lambda i: (i, 0)),
            pl.BlockSpec((ATTN_KV_HEADS, tm, HEAD_DIM), lambda i: (0, i, 0)),
            pl.BlockSpec((ATTN_KV_HEADS, tm, HEAD_DIM), lambda i: (0, i, 0)),
        ],
        out_shape=[
            jax.ShapeDtypeStruct((t, RWKV_COLS), F32),
            jax.ShapeDtypeStruct((t, ATTN_WIDTH), BF16),
            jax.ShapeDtypeStruct((ATTN_KV_HEADS, t, HEAD_DIM), BF16),
            jax.ShapeDtypeStruct((ATTN_KV_HEADS, t, HEAD_DIM), BF16),
        ],
        compiler_params=_cparams(("parallel",)),
        name="in_proj",
    )(x2, norm1, w_in, qk_gain, cos_t, sin_t, seg_qk)


def _attn_kernel(q_ref, k_ref, v_ref, o_ref, qs_ref, m_ref, l_ref, acc_ref, *, tq):
    ki = pl.program_id(2)

    @pl.when(ki == 0)
    def _():
        q = q_ref[...]
        for j in range(ATTN_KV_HEADS):
            for g in range(KV_GROUP):
                h = j * KV_GROUP + g
                qs_ref[j, g * tq:(g + 1) * tq, :] = q[:, h * HEAD_DIM:(h + 1) * HEAD_DIM]
        m_ref[...] = jnp.full(m_ref.shape, -jnp.inf, F32)
        l_ref[...] = jnp.zeros(l_ref.shape, F32)
        acc_ref[...] = jnp.zeros(acc_ref.shape, F32)

    for j in range(ATTN_KV_HEADS):
        s = _dot_nt(qs_ref[j], k_ref[j])
        m_old = m_ref[j]
        m_new = jnp.maximum(m_old, jnp.max(s, axis=-1, keepdims=True))
        alpha = jnp.exp(m_old - m_new)
        p = jnp.exp(s - m_new)
        l_ref[j] = alpha * l_ref[j] + jnp.sum(p, axis=-1, keepdims=True)
        acc_ref[j] = alpha * acc_ref[j] + _dot(p.astype(BF16), v_ref[j])
        m_ref[j] = m_new

    @pl.when(ki == pl.num_programs(2) - 1)
    def _():
        for j in range(ATTN_KV_HEADS):
            o = acc_ref[j] / l_ref[j]
            for g in range(KV_GROUP):
                h = j * KV_GROUP + g
                o_ref[:, h * HEAD_DIM:(h + 1) * HEAD_DIM] = o[g * tq:(g + 1) * tq, :].astype(o_ref.dtype)


def _attention(q, k, v, b, n, tq, tk):
    t = b * n
    nq, nk = n // tq, n // tk
    rows = KV_GROUP * tq
    return pl.pallas_call(
        functools.partial(_attn_kernel, tq=tq),
        grid=(b, nq, nk),
        in_specs=[
            pl.BlockSpec((tq, ATTN_WIDTH), lambda bi, qi, ki: (bi * nq + qi, 0)),
            pl.BlockSpec((ATTN_KV_HEADS, tk, HEAD_DIM), lambda bi, qi, ki: (0, bi * nk + ki, 0)),
            pl.BlockSpec((ATTN_KV_HEADS, tk, HEAD_DIM), lambda bi, qi, ki: (0, bi * nk + ki, 0)),
        ],
        out_specs=pl.BlockSpec((tq, ATTN_WIDTH), lambda bi, qi, ki: (bi * nq + qi, 0)),
        out_shape=jax.ShapeDtypeStruct((t, ATTN_WIDTH), BF16),
        scratch_shapes=[
            pltpu.VMEM((ATTN_KV_HEADS, rows, HEAD_DIM), BF16),
            pltpu.VMEM((ATTN_KV_HEADS, rows, 1), F32),
            pltpu.VMEM((ATTN_KV_HEADS, rows, 1), F32),
            pltpu.VMEM((ATTN_KV_HEADS, rows, HEAD_DIM), F32),
        ],
        compiler_params=_cparams(("parallel", "parallel", "arbitrary")),
        name="attention",
    )(q, k, v)


def _shift_feat(p, prev_row, next_row, taps):
    c = p.shape[0]
    rows = lax.broadcasted_iota(I32, p.shape, 0)
    prev = jnp.where(rows == 0, prev_row, pltpu.roll(p, 1, 0))
    nxt = jnp.where(rows == c - 1, next_row, pltpu.roll(p, c - 1, 0))
    return taps[0:1] * prev + taps[1:2] * p + taps[2:3] * nxt


def _split_feat(feat):
    r = feat[:, :RWKV_WIDTH]
    k = feat[:, RWKV_WIDTH:2 * RWKV_WIDTH]
    v = feat[:, 2 * RWKV_WIDTH:3 * RWKV_WIDTH]
    o = 3 * RWKV_WIDTH
    wd = feat[:, o:o + W_LORA]
    ad = feat[:, o + W_LORA:o + W_LORA + A_LORA]
    gd = feat[:, o + W_LORA + A_LORA:]
    return r, k, v, wd, ad, gd


def _in_context_rate(ad_b, a0_row, a_up):
    return _sigmoid(a0_row + _dot(ad_b, a_up))


def _halo_rows(hp_ref, hn_ref, is_first, is_last):
    prev_row = jnp.where(is_first, 0.0, hp_ref[7:8, :])
    next_row = jnp.where(is_last, 0.0, hn_ref[0:1, :])
    return prev_row, next_row


def _masked_stack(x):
    lane = lax.broadcasted_iota(I32, x.shape, 1)
    return jnp.concatenate([jnp.where(lane < HEAD_DIM, x, 0.0), jnp.where(lane >= HEAD_DIM, x, 0.0)], axis=0)


def _wkv_chunk(r, kd, v, an, bb, lw, rev, s_ref, d):
    c = CHUNK
    ti = lax.broadcasted_iota(I32, (c, c), 0)
    si = lax.broadcasted_iota(I32, (c, c), 1)
    incl = (si >= ti) if rev else (si <= ti)
    l_incl = incl.astype(BF16)
    lw_hi, lw_mid, lw_lo = _split3(lw)
    cum = _dot(l_incl, lw_hi) + _dot(l_incl, lw_mid) + _dot(l_incl, lw_lo)
    tot = jnp.sum(lw, axis=0, keepdims=True)
    e_neg = jnp.exp(-cum)
    e_tot = jnp.exp(tot - cum)
    a_t = an * jnp.exp(cum - lw)
    r_t = r * jnp.exp(cum)
    b_t = bb * e_neg
    k_t = kd * e_neg
    b_h = bb * e_tot
    k_h = kd * e_tot
    g_c = jnp.exp(tot)

    rr = lax.broadcasted_iota(I32, (PAIR, PAIR), 0) & (c - 1)
    cc = lax.broadcasted_iota(I32, (PAIR, PAIR), 1) & (c - 1)
    tri_incl = (cc >= rr) if rev else (cc <= rr)
    tri_strict = (cc > rr) if rev else (cc < rr)
    eye = (lax.broadcasted_iota(I32, (PAIR, PAIR), 0) == lax.broadcasted_iota(I32, (PAIR, PAIR), 1)).astype(F32)

    ys = []
    for p in range(N_PAIRS):
        sl = slice(p * PAIR, (p + 1) * PAIR)
        a_ms = _masked_stack(a_t[:, sl]).astype(BF16)
        r_msf = _masked_stack(r_t[:, sl])
        b_ms = _masked_stack(b_t[:, sl]).astype(BF16)
        k_ms = _masked_stack(k_t[:, sl]).astype(BF16)
        bh_ms = _masked_stack(b_h[:, sl]).astype(BF16)
        kh_ms = _masked_stack(k_h[:, sl]).astype(BF16)
        v_ms = _masked_stack(v[:, sl]).astype(BF16)

        gram = _dot_nt(jnp.concatenate([a_ms, r_msf.astype(BF16)], axis=0),
                       jnp.concatenate([b_ms, k_ms], axis=0))
        m_ab = jnp.where(tri_strict, gram[:PAIR, :PAIR], 0.0)
        m_ak = jnp.where(tri_strict, gram[:PAIR, PAIR:], 0.0).astype(BF16)
        m_rb = jnp.where(tri_incl, gram[PAIR:, :PAIR], 0.0).astype(BF16)
        m_rk = jnp.where(tri_incl, gram[PAIR:, PAIR:], 0.0).astype(BF16)

        t_inv = eye + m_ab
        xp = m_ab
        for _ in range(int(math.log2(c)) - 1):
            xb = xp.astype(BF16)
            xp = _dot(xb, xb)
            t_inv = t_inv + _dot(t_inv.astype(BF16), xp.astype(BF16))
        t_inv = t_inv.astype(BF16)

        mv = _dot(m_ak, v_ms)
        ta = _dot(t_inv, jnp.concatenate([a_ms, mv.astype(BF16)], axis=1))
        ta_b = ta.astype(BF16)
        gh = _dot(ta.T.astype(BF16), bh_ms)
        vk = _dot(v_ms.astype(F32).T.astype(BF16), kh_ms)
        mr = _dot(m_rb, ta_b)
        q_h = (r_msf + mr[:, :PAIR]).astype(BF16)
        y_in = mr[:, PAIR:] + _dot(m_rk, v_ms)

        s_in = s_ref[d, p]
        s_b = s_in.astype(BF16)
        y_ms = _dot_nt(q_h, s_b) + y_in
        ys.append(y_ms[:c] + y_ms[c:])
        s_ref[d, p] = s_in * g_c[:, sl] + _dot(s_b, gh[:PAIR].astype(BF16)) + gh[PAIR:] + vk
    return jnp.concatenate(ys, axis=1)


def _wkv_kernel(pf_ref, hpf_ref, hnf_ref, pr_ref, hpr_ref, hnr_ref,
                taps_ref, w0_ref, wup_ref, a0_ref, aup_ref, kk_ref, ka_ref, seg_ref,
                yf_ref, yr_ref, s_ref):
    c = pl.program_id(1)
    nc = pl.num_programs(1)

    @pl.when(c == 0)
    def _():
        s_ref[...] = jnp.zeros(s_ref.shape, F32)

    taps = taps_ref[...]
    seg = seg_ref[...]
    for d, (p_ref, hp_ref, hn_ref, y_ref) in enumerate(((pf_ref, hpf_ref, hnf_ref, yf_ref),
                                                        (pr_ref, hpr_ref, hnr_ref, yr_ref))):
        chunk = c if d == 0 else nc - 1 - c
        prev_row, next_row = _halo_rows(hp_ref, hn_ref, chunk == 0, chunk == nc - 1)
        feat = _shift_feat(p_ref[...], prev_row, next_row, taps)
        r, k, v, wd, ad, _ = _split_feat(feat)
        kk = k * kk_ref[...]
        kk = kk * lax.rsqrt(jnp.maximum(_seg_sum(kk * kk, seg), KK_EPS_SQ))
        tw = jnp.tanh(wd).astype(BF16)
        lw = -DECAY_SCALE * _sigmoid(w0_ref[d:d + 1, :] + _dot(tw, wup_ref[d]))
        a = _in_context_rate(ad.astype(BF16), a0_ref[d:d + 1, :], aup_ref[d])
        kd = k * (1.0 + (a - 1.0) * ka_ref[...])
        y_ref[...] = _wkv_chunk(r, kd, v, -kk, kk * a, lw, d == 1, s_ref, d)


def _wkv(feat, taps, w0, w_up, a0, a_up, k_k, k_a, seg_r, b, n):
    t = b * n
    nc = n // CHUNK
    c8 = CHUNK // 8
    last8 = t // 8 - 1

    def fwd(bi, ci):
        return bi * nc + ci

    def rev(bi, ci):
        return bi * nc + (nc - 1 - ci)

    def specs(chunk_of):
        return [
            pl.BlockSpec((CHUNK, RWKV_COLS), lambda bi, ci: (chunk_of(bi, ci), 0)),
            pl.BlockSpec((8, RWKV_COLS), lambda bi, ci: (jnp.maximum(chunk_of(bi, ci) * c8 - 1, 0), 0)),
            pl.BlockSpec((8, RWKV_COLS), lambda bi, ci: (jnp.minimum((chunk_of(bi, ci) + 1) * c8, last8), 0)),
        ]

    return pl.pallas_call(
        _wkv_kernel,
        grid=(b, nc),
        in_specs=specs(fwd) + specs(rev) + [
            _full((3, RWKV_COLS)),
            _full((2, RWKV_WIDTH)),
            _full((2, W_LORA, RWKV_WIDTH)),
            _full((2, RWKV_WIDTH)),
            _full((2, A_LORA, RWKV_WIDTH)),
            _full((1, RWKV_WIDTH)),
            _full((1, RWKV_WIDTH)),
            _full((RWKV_WIDTH, RWKV_WIDTH)),
        ],
        out_specs=[
            pl.BlockSpec((CHUNK, RWKV_WIDTH), lambda bi, ci: (fwd(bi, ci), 0)),
            pl.BlockSpec((CHUNK, RWKV_WIDTH), lambda bi, ci: (rev(bi, ci), 0)),
        ],
        out_shape=[jax.ShapeDtypeStruct((t, RWKV_WIDTH), F32)] * 2,
        scratch_shapes=[pltpu.VMEM((2, N_PAIRS, PAIR, PAIR), F32)],
        compiler_params=_cparams(("parallel", "arbitrary")),
        name="wkv",
    )(feat, feat, feat, feat, feat, feat, taps, w0, w_up, a0, a_up, k_k, k_a, seg_r)


def _rwkv_post_kernel(p_ref, hp_ref, hn_ref, yf_ref, yr_ref, taps_ref, a0_ref, aup_ref, ka_ref, rk_ref,
                      gup_ref, lng_ref, lnb_ref, seg_ref, o_ref, *, blocks_per_seq):
    i = pl.program_id(0)
    pos = i % blocks_per_seq
    prev_row, next_row = _halo_rows(hp_ref, hn_ref, pos == 0, pos == blocks_per_seq - 1)
    feat = _shift_feat(p_ref[...], prev_row, next_row, taps_ref[...])
    r, k, v, _, ad, gd = _split_feat(feat)
    seg = seg_ref[...]
    ad_b = ad.astype(BF16)
    a_sum = (_in_context_rate(ad_b, a0_ref[0:1, :], aup_ref[0])
             + _in_context_rate(ad_b, a0_ref[1:2, :], aup_ref[1]))
    k_bonus = k * (1.0 + (0.5 * a_sum - 1.0) * ka_ref[...])
    gate = _dot(_sigmoid(gd).astype(BF16), gup_ref[...])
    y = yf_ref[...] + yr_ref[...]
    mu = _seg_sum(y, seg) * (1.0 / HEAD_DIM)
    yc = y - mu
    var = _seg_sum(yc * yc, seg) * (1.0 / HEAD_DIM)
    yn = yc * lax.rsqrt(var + GN_EPS) * lng_ref[...] + lnb_ref[...]
    bonus = _seg_sum(r * k_bonus * rk_ref[...], seg) * v
    o_ref[...] = ((yn + bonus) * gate).astype(o_ref.dtype)


def _rwkv_post(feat, y_f, y_r, taps, a0, a_up, k_a, r_k, g_up, ln_g, ln_b, seg_r, n, tm):
    t = feat.shape[0]
    bps = n // tm
    t8 = tm // 8
    last8 = t // 8 - 1
    row = lambda i: (i, 0)
    return pl.pallas_call(
        functools.partial(_rwkv_post_kernel, blocks_per_seq=bps),
        grid=(t // tm,),
        in_specs=[
            pl.BlockSpec((tm, RWKV_COLS), row),
            pl.BlockSpec((8, RWKV_COLS), lambda i: (jnp.maximum(i * t8 - 1, 0), 0)),
            pl.BlockSpec((8, RWKV_COLS), lambda i: (jnp.minimum((i + 1) * t8, last8), 0)),
            pl.BlockSpec((tm, RWKV_WIDTH), row),
            pl.BlockSpec((tm, RWKV_WIDTH), row),
            _full((3, RWKV_COLS)),
            _full((2, RWKV_WIDTH)),
            _full((2, A_LORA, RWKV_WIDTH)),
            _full((1, RWKV_WIDTH)),
            _full((1, RWKV_WIDTH)),
            _full((G_LORA, RWKV_WIDTH)),
            _full((1, RWKV_WIDTH)),
            _full((1, RWKV_WIDTH)),
            _full((RWKV_WIDTH, RWKV_WIDTH)),
        ],
        out_specs=pl.BlockSpec((tm, RWKV_WIDTH), row),
        out_shape=jax.ShapeDtypeStruct((t, RWKV_WIDTH), BF16),
        compiler_params=_cparams(("parallel",)),
        name="rwkv_post",
    )(feat, feat, feat, y_f, y_r, taps, a0, a_up, k_a, r_k, g_up, ln_g, ln_b, seg_r)


def _out_proj_kernel(x_ref, ro_ref, ao_ref, wo_ref, g2_ref, wrh_ref, wrl_ref, h_ref, xn_ref, aff_ref):
    mix = _dot(ro_ref[...], wo_ref[:RWKV_WIDTH, :]) + _dot(ao_ref[...], wo_ref[RWKV_WIDTH:, :])
    h = x_ref[...] + mix
    h_ref[...] = h
    ms = jnp.mean(h * h, axis=-1, keepdims=True)
    xn = h * lax.rsqrt(ms + NORM_EPS) * g2_ref[...]
    xn_ref[...] = xn
    xh, xl = _split2(xn)
    wrh = wrh_ref[...]
    logits = _dot(xh, wrh) + _dot(xl, wrh) + _dot(xh, wrl_ref[...])
    lane = lax.broadcasted_iota(I32, logits.shape, 1)
    logits = jnp.where(lane < N_EXPERTS, logits, -jnp.inf)
    e = jnp.exp(logits - jnp.max(logits, axis=-1, keepdims=True))
    aff = e / jnp.sum(e, axis=-1, keepdims=True)
    aff_ref[...] = aff.T[:N_EXPERTS, :]


def _out_proj(x2, rwkv_out, attn_out, w_out, norm2, wr_hi, wr_lo, tm):
    t = x2.shape[0]
    row = lambda i: (i, 0)
    return pl.pallas_call(
        _out_proj_kernel,
        grid=(t // tm,),
        in_specs=[
            pl.BlockSpec((tm, D_MODEL), row),
            pl.BlockSpec((tm, RWKV_WIDTH), row),
            pl.BlockSpec((tm, ATTN_WIDTH), row),
            _full((D_MODEL, D_MODEL)),
            _full((1, D_MODEL)),
            _full((D_MODEL, LANES)),
            _full((D_MODEL, LANES)),
        ],
        out_specs=[
            pl.BlockSpec((tm, D_MODEL), row),
            pl.BlockSpec((tm, D_MODEL), row),
            pl.BlockSpec((N_EXPERTS, tm), lambda i: (0, i)),
        ],
        out_shape=[
            jax.ShapeDtypeStruct((t, D_MODEL), F32),
            jax.ShapeDtypeStruct((t, D_MODEL), F32),
            jax.ShapeDtypeStruct((N_EXPERTS, t), F32),
        ],
        compiler_params=_cparams(("parallel",)),
        name="out_proj",
    )(x2, rwkv_out, attn_out, w_out, norm2, wr_hi, wr_lo)


def _topk_kernel(aff_ref, inv_ref, rb_ref, *, cap, t, tb):
    nrows = t // LANES
    nblk = t // tb
    rows_per_blk = tb // LANES

    def bits(sl):
        return pltpu.bitcast(aff_ref[:, sl], I32)

    def bisect(i, prefix):
        cand = prefix | jnp.left_shift(jnp.int32(1), 30 - i)
        cnt = jnp.sum((bits(slice(None)) >= cand).astype(I32), axis=1, keepdims=True)
        return jnp.where(cnt >= cap, cand, prefix)

    tau = lax.fori_loop(0, 31, bisect, jnp.zeros((N_EXPERTS, 1), I32))
    n_gt = jnp.sum((bits(slice(None)) > tau).astype(I32), axis=1, keepdims=True)
    need = (cap - n_gt).astype(F32)

    upper = (lax.broadcasted_iota(I32, (tb, tb), 0) < lax.broadcasted_iota(I32, (tb, tb), 1)).astype(BF16)
    tok_row = lax.broadcasted_iota(I32, (tb, nrows), 0) >> 7
    out_row = lax.broadcasted_iota(I32, (tb, nrows), 1)

    def block(b, carry):
        c_eq, c_pos, row_cnt = carry
        start = pl.multiple_of(b * tb, tb)
        xs = bits(pl.ds(start, tb))
        gt = xs > tau
        eq = xs == tau
        eq_b = eq.astype(BF16)
        rank_eq = c_eq + _dot(eq_b, upper)
        sel = gt | (eq & (rank_eq < need))
        sel_b = sel.astype(BF16)
        pos = c_pos + _dot(sel_b, upper)
        inv_ref[:, pl.ds(start, tb)] = jnp.where(sel, pos.astype(I32), -1)
        place = (out_row == tok_row + b * rows_per_blk).astype(BF16)
        row_cnt = row_cnt + _dot(sel_b, place)
        c_eq = c_eq + jnp.sum(eq_b.astype(F32), axis=1, keepdims=True)
        c_pos = c_pos + jnp.sum(sel_b.astype(F32), axis=1, keepdims=True)
        return c_eq, c_pos, row_cnt

    zero = jnp.zeros((N_EXPERTS, 1), F32)
    _, _, row_cnt = lax.fori_loop(0, nblk, block, (zero, zero, jnp.zeros((N_EXPERTS, nrows), F32)))
    upper_r = (lax.broadcasted_iota(I32, (nrows, nrows), 0) < lax.broadcasted_iota(I32, (nrows, nrows), 1)).astype(BF16)
    rb_ref[...] = _dot(row_cnt.astype(BF16), upper_r).astype(I32)


def _topk(aff_t, cap):
    t = aff_t.shape[1]
    tb = min(512, t)
    return pl.pallas_call(
        functools.partial(_topk_kernel, cap=cap, t=t, tb=tb),
        in_specs=[_full((N_EXPERTS, t))],
        out_specs=[_full((N_EXPERTS, t)), _full((N_EXPERTS, t // LANES))],
        out_shape=[jax.ShapeDtypeStruct((N_EXPERTS, t), I32), jax.ShapeDtypeStruct((N_EXPERTS, t // LANES), I32)],
        compiler_params=pltpu.CompilerParams(vmem_limit_bytes=VMEM_LIMIT),
        name="topk",
    )(aff_t)


def _compact_kernel(rb_ref, inv_ref, aff_ref, idx_ref, gate_ref, acc_ref, *, nrows, ntiles):
    e = pl.program_id(0)
    acc_ref[...] = jnp.zeros(acc_ref.shape, F32)
    sub = lax.broadcasted_iota(I32, (8, LANES), 0)
    lane_tok = lax.broadcasted_iota(I32, (8, LANES), 1)
    slot_off = lax.broadcasted_iota(I32, (LANES, LANES), 0)

    def body(r, _):
        start = pl.multiple_of(r * LANES, LANES)
        inv_row = inv_ref[0, :, pl.ds(start, LANES)]
        g_hi, g_mid, g_lo = _split3(aff_ref[0, :, pl.ds(start, LANES)])
        tok = lane_tok + r * LANES
        lmat = jnp.where(sub == 0, (tok >> 8).astype(F32),
                         jnp.where(sub == 1, (tok & 255).astype(F32),
                                   jnp.where(sub == 2, g_hi.astype(F32),
                                             jnp.where(sub == 3, g_mid.astype(F32),
                                                       jnp.where(sub == 4, g_lo.astype(F32), 0.0))))).astype(BF16)
        q0 = rb_ref[e * nrows + r] >> 7
        inv_b = jnp.broadcast_to(inv_row, (LANES, LANES))
        for dq in range(2):
            onehot_t = (inv_b == slot_off + (q0 + dq) * LANES).astype(BF16)
            acc_ref[q0 + dq] += _dot_nt(lmat, onehot_t)
        return 0

    lax.fori_loop(0, nrows, body, 0)
    part = lambda s: acc_ref[0:ntiles, s, :]
    idx_ref[0] = (part(0) * 256.0 + part(1)).astype(I32)
    gate_ref[0] = part(2) + part(3) + part(4)


def _compact(rb_flat, inv, aff_t, cap):
    t = inv.shape[1]
    nrows = t // LANES
    ntiles = cap // LANES
    return pl.pallas_call(
        functools.partial(_compact_kernel, nrows=nrows, ntiles=ntiles),
        grid_spec=pltpu.PrefetchScalarGridSpec(
            num_scalar_prefetch=1,
            grid=(N_EXPERTS,),
            in_specs=[
                pl.BlockSpec((1, 1, t), lambda e, rb: (e, 0, 0)),
                pl.BlockSpec((1, 1, t), lambda e, rb: (e, 0, 0)),
            ],
            out_specs=[
                pl.BlockSpec((1, ntiles, LANES), lambda e, rb: (e, 0, 0)),
                pl.BlockSpec((1, ntiles, LANES), lambda e, rb: (e, 0, 0)),
            ],
            scratch_shapes=[pltpu.VMEM((ntiles + 2, 8, LANES), F32)],
        ),
        out_shape=[jax.ShapeDtypeStruct((N_EXPERTS, ntiles, LANES), I32),
                   jax.ShapeDtypeStruct((N_EXPERTS, ntiles, LANES), F32)],
        compiler_params=_cparams(("arbitrary",)),
        name="compact",
    )(rb_flat, inv.reshape(N_EXPERTS, 1, t), aff_t.reshape(N_EXPERTS, 1, t))


def _expert_kernel(idx_ref, gate_ref, x_hbm, y_in_hbm, wg_ref, wu_ref, wd_ref, y_hbm, xbuf, ybuf, sems, *, ts):
    del y_in_hbm

    def x_copy(j):
        return pltpu.make_async_copy(x_hbm.at[pl.ds(idx_ref[0, 0, j], 1)], xbuf.at[pl.ds(j, 1)], sems.at[0])

    def y_load(j):
        return pltpu.make_async_copy(y_hbm.at[pl.ds(idx_ref[0, 0, j], 1)], ybuf.at[pl.ds(j, 1)], sems.at[1])

    def y_store(j):
        return pltpu.make_async_copy(ybuf.at[pl.ds(j, 1)], y_hbm.at[pl.ds(idx_ref[0, 0, j], 1)], sems.at[2])

    def for_rows(fn):
        def body(j, _):
            fn(j)
            return 0
        lax.fori_loop(0, ts, body, 0, unroll=8)

    def start_loads(j):
        x_copy(j).start()
        y_load(j).start()

    for_rows(start_loads)
    for_rows(lambda j: x_copy(j).wait())
    xs = xbuf[...].astype(BF16)
    gp = _dot(xs, wg_ref[0])
    hid = (gp * _sigmoid(gp) * _dot(xs, wu_ref[0])).astype(BF16)
    out = _dot(hid, wd_ref[0])

    eye = (lax.broadcasted_iota(I32, (ts, ts), 0) == lax.broadcasted_iota(I32, (ts, ts), 1)).astype(BF16)
    gcol = jnp.zeros((ts, LANES), F32)
    for part in _split3(gate_ref[0]):
        gcol = gcol + _dot_nt(eye, jnp.broadcast_to(part, (LANES, ts)))
    for_rows(lambda j: y_load(j).wait())
    ybuf[...] = ybuf[...] + out * gcol[:, 0:1]
    for_rows(lambda j: y_store(j).start())
    for_rows(lambda j: y_store(j).wait())


def _experts(idx, gates, xn, h, w_gate, w_up, w_down, cap, ts):
    t = xn.shape[0]
    nt = cap // ts
    idx3 = idx.reshape(N_EXPERTS * nt, 1, ts)
    gate3 = gates.reshape(N_EXPERTS * nt, 1, ts)
    wspec = lambda a, b: pl.BlockSpec((1, a, b), lambda e, i: (e, 0, 0))
    return pl.pallas_call(
        functools.partial(_expert_kernel, ts=ts),
        grid=(N_EXPERTS, nt),
        in_specs=[
            pl.BlockSpec((1, 1, ts), lambda e, i: (e * nt + i, 0, 0), memory_space=pltpu.SMEM),
            pl.BlockSpec((1, 1, ts), lambda e, i: (e * nt + i, 0, 0)),
            pl.BlockSpec(memory_space=pl.ANY),
            pl.BlockSpec(memory_space=pl.ANY),
            wspec(D_MODEL, D_EXPERT),
            wspec(D_MODEL, D_EXPERT),
            wspec(D_EXPERT, D_MODEL),
        ],
        out_specs=pl.BlockSpec(memory_space=pl.ANY),
        out_shape=jax.ShapeDtypeStruct((t, D_MODEL), F32),
        scratch_shapes=[
            pltpu.VMEM((ts, D_MODEL), F32),
            pltpu.VMEM((ts, D_MODEL), F32),
            pltpu.SemaphoreType.DMA((3,)),
        ],
        input_output_aliases={3: 0},
        compiler_params=_cparams(("arbitrary", "arbitrary")),
        name="experts",
    )(idx3, gate3, xn, h, w_gate, w_up, w_down)


def _rope_tables(n):
    pos = jnp.arange(n, dtype=F32)
    row = jnp.floor(pos / GRID_W)
    col = pos - row * GRID_W
    inv_freq = ROPE_THETA ** (-jnp.arange(ROPE_PAIRS, dtype=F32) / ROPE_PAIRS)
    ang_r = row[:, None] * inv_freq
    ang_c = col[:, None] * inv_freq
    cos = jnp.concatenate([jnp.cos(ang_r), jnp.cos(ang_r), jnp.cos(ang_c), jnp.cos(ang_c)], axis=1)
    sin = jnp.concatenate([-jnp.sin(ang_r), jnp.sin(ang_r), -jnp.sin(ang_c), jnp.sin(ang_c)], axis=1)
    reps = LANES // HEAD_DIM
    return jnp.tile(cos, (1, reps)), jnp.tile(sin, (1, reps))


def _block_diag_ones(width):
    i = jnp.arange(width) // HEAD_DIM
    return (i[:, None] == i[None, :]).astype(BF16)


def _pick(n, prefs):
    for p in prefs:
        if n % p == 0:
            return p
    raise ValueError(f"no block size in {prefs} divides {n}")


def _layer(x, prm):
    b, n, _ = x.shape
    t = b * n
    cap = CAPACITY_FACTOR * t // N_EXPERTS
    assert n % CHUNK == 0 and n % GRID_W == 0 and t % LANES == 0 and cap % LANES == 0
    x2 = x.reshape(t, D_MODEL)
    cos_t, sin_t = _rope_tables(n)

    tm = _pick(n, (512, 256, 128, 64))
    feat, q, k, v = _in_proj(x2, prm["norm1"], prm["w_in"], prm["qk_gain"], cos_t, sin_t, prm["seg_qk"], n, tm)

    tq = _pick(n, (256, 128, 64))
    tk = _pick(n, (1024, 512, 256, 128, 64))
    attn_out = _attention(q, k, v, b, n, tq, tk)

    y_f, y_r = _wkv(feat, prm["taps"], prm["w0"], prm["w_up"], prm["a0"], prm["a_up"], prm["k_k"], prm["k_a"],
                    prm["seg_r"], b, n)
    tp = _pick(n, (256, 128, 64))
    rwkv_out = _rwkv_post(feat, y_f, y_r, prm["taps"], prm["a0"], prm["a_up"], prm["k_a"], prm["r_k"],
                          prm["g_up"], prm["ln_g"], prm["ln_b"], prm["seg_r"], n, tp)

    h, xn, aff_t = _out_proj(x2, rwkv_out, attn_out, prm["w_out"], prm["norm2"], prm["wr_hi"], prm["wr_lo"], tm)

    inv, rb = _topk(aff_t, cap)
    idx, gates = _compact(rb.reshape(-1), inv, aff_t, cap)
    ts = _pick(cap, (256, 128))
    y = _experts(idx.reshape(N_EXPERTS, cap), gates.reshape(N_EXPERTS, cap), xn, h,
                 prm["w_gate"], prm["w_up_e"], prm["w_down"], cap, ts)
    return y.reshape(b, n, D_MODEL)


def _prepare(norm1, w_in, shift_taps, w0, w_lora_up, a0, a_lora_up, k_k, k_a, r_k, g_lora_up, ln_g, ln_b,
             q_norm, k_norm, w_out, norm2, w_router, expert_w_gate, expert_w_up, expert_w_down):
    l = 0
    wr = jnp.pad(w_router[l], ((0, 0), (0, LANES - N_EXPERTS)))
    wr_hi = wr.astype(BF16)
    wr_lo = (wr - wr_hi.astype(F32)).astype(BF16)
    qk_gain = jnp.concatenate([jnp.tile(q_norm[l], ATTN_Q_HEADS), jnp.tile(k_norm[l], ATTN_KV_HEADS)])
    return dict(
        norm1=norm1[l].reshape(1, D_MODEL),
        w_in=w_in[l].astype(BF16),
        taps=shift_taps[l],
        w0=w0[l],
        w_up=w_lora_up[l].astype(BF16),
        a0=a0[l],
        a_up=a_lora_up[l].astype(BF16),
        k_k=k_k[l].reshape(1, RWKV_WIDTH),
        k_a=k_a[l].reshape(1, RWKV_WIDTH),
        r_k=r_k[l].reshape(1, RWKV_WIDTH),
        g_up=g_lora_up[l].astype(BF16),
        ln_g=ln_g[l].reshape(1, RWKV_WIDTH),
        ln_b=ln_b[l].reshape(1, RWKV_WIDTH),
        qk_gain=qk_gain.reshape(1, QK_COLS),
        w_out=w_out[l].astype(BF16),
        norm2=norm2[l].reshape(1, D_MODEL),
        wr_hi=wr_hi,
        wr_lo=wr_lo,
        w_gate=expert_w_gate[l].astype(BF16),
        w_up_e=expert_w_up[l].astype(BF16),
        w_down=expert_w_down[l].astype(BF16),
        seg_qk=_block_diag_ones(QK_COLS),
        seg_r=_block_diag_ones(RWKV_WIDTH),
    )


def kernel(x_prompt, x_sample, norm1, w_in, shift_taps, w0, w_lora_up, a0, a_lora_up, k_k, k_a, r_k, g_lora_up,
           ln_g, ln_b, q_norm, k_norm, w_out, norm2, w_router, expert_w_gate, expert_w_up, expert_w_down):
    assert norm1.shape[0] == 1, "single-layer trunk"
    prm = _prepare(norm1, w_in, shift_taps, w0, w_lora_up, a0, a_lora_up, k_k, k_a, r_k, g_lora_up, ln_g, ln_b,
                   q_norm, k_norm, w_out, norm2, w_router, expert_w_gate, expert_w_up, expert_w_down)
    return (_layer(x_prompt, prm), _layer(x_sample, prm))
```

```python
import functools
import math

import jax
import jax.numpy as jnp
from jax import lax
from jax.experimental import pallas as pl
from jax.experimental.pallas import tpu as pltpu

F32 = jnp.float32
BF16 = jnp.bfloat16
I32 = jnp.int32

D_MODEL = 1024
HEAD_DIM = 64
RWKV_WIDTH = 512
RWKV_HEADS = RWKV_WIDTH // HEAD_DIM
ATTN_WIDTH = D_MODEL - RWKV_WIDTH
ATTN_Q_HEADS = ATTN_WIDTH // HEAD_DIM
ATTN_KV_HEADS = 2
KV_GROUP = ATTN_Q_HEADS // ATTN_KV_HEADS
KV_WIDTH = ATTN_KV_HEADS * HEAD_DIM
W_LORA = 64
A_LORA = 64
G_LORA = 128
RWKV_COLS = 3 * RWKV_WIDTH + W_LORA + A_LORA + G_LORA
IN_COLS = RWKV_COLS + ATTN_WIDTH + 2 * KV_WIDTH
QK_COLS = ATTN_WIDTH + KV_WIDTH
GRID_W = 64
ROPE_THETA = 10000.0
ROPE_PAIRS = HEAD_DIM // 4
N_EXPERTS = 16
CAPACITY_FACTOR = 2
D_EXPERT = 1024
NORM_EPS = 1e-6
GN_EPS = 64e-5
KK_EPS_SQ = 1e-24
DECAY_SCALE = math.exp(-0.5)
Q_SCALE = math.log2(math.e) / math.sqrt(HEAD_DIM)

LANES = 128
CHUNK = 64
PAIR = 2 * HEAD_DIM
N_PAIRS = RWKV_WIDTH // PAIR
QSUB = 256
KEY_CHUNK = 1024
VROWS = 80
CHUNK_ALIGN = 16
CHUNK_ROWS = LANES + CHUNK_ALIGN
VMEM_LIMIT = 48 * 1024 * 1024


def _cparams(sem):
    return pltpu.CompilerParams(dimension_semantics=sem, vmem_limit_bytes=VMEM_LIMIT)


def _full(shape):
    nd = len(shape)
    return pl.BlockSpec(shape, lambda *_: (0,) * nd)


def _dot(a, b):
    return jnp.dot(a, b, preferred_element_type=F32)


def _dot_nt(a, b):
    return lax.dot_general(a, b, (((1,), (1,)), ((), ())), preferred_element_type=F32)


def _split2(x):
    hi = x.astype(BF16)
    lo = (x - hi.astype(F32)).astype(BF16)
    return hi, lo


def _split3(x):
    hi = x.astype(BF16)
    r1 = x - hi.astype(F32)
    mid = r1.astype(BF16)
    lo = (r1 - mid.astype(F32)).astype(BF16)
    return hi, mid, lo


def _seg_sum(x, seg):
    hi, lo = _split2(x)
    return _dot(hi, seg) + _dot(lo, seg)


def _sigmoid(x):
    return 1.0 / (1.0 + jnp.exp(-x))


def _in_proj_kernel(x_ref, g_ref, w_ref, qkg_ref, cos_ref, sin_ref, seg_ref,
                    feat_ref, q_ref, k_ref, vt_ref):
    x = x_ref[...]
    ms = jnp.mean(x * x, axis=-1, keepdims=True)
    xn = (x * lax.rsqrt(ms + NORM_EPS) * g_ref[...]).astype(BF16)
    p = _dot(xn, w_ref[...])
    feat_ref[...] = p[:, :RWKV_COLS]
    qk = p[:, RWKV_COLS:RWKV_COLS + QK_COLS]
    ssq = _seg_sum(qk * qk, seg_ref[...])
    qkn = qk * lax.rsqrt(ssq * (1.0 / HEAD_DIM) + NORM_EPS) * qkg_ref[...]
    reps = QK_COLS // LANES
    cos = jnp.concatenate([cos_ref[...]] * reps, axis=1)
    sin = jnp.concatenate([sin_ref[...]] * reps, axis=1)
    lane = lax.broadcasted_iota(I32, qkn.shape, 1)
    first = (lane & (2 * ROPE_PAIRS - 1)) < ROPE_PAIRS
    partner = jnp.where(first, pltpu.roll(qkn, QK_COLS - ROPE_PAIRS, 1), pltpu.roll(qkn, ROPE_PAIRS, 1))
    rot = qkn * cos + partner * sin
    q_ref[...] = (rot[:, :ATTN_WIDTH] * Q_SCALE).astype(BF16)
    kr = rot[:, ATTN_WIDTH:].astype(BF16)
    vt = p[:, RWKV_COLS + QK_COLS:].T
    row = lax.broadcasted_iota(I32, vt.shape, 0)
    for j in range(ATTN_KV_HEADS):
        k_ref[j] = kr[:, j * HEAD_DIM:(j + 1) * HEAD_DIM]
        vj = vt if j == 0 else pltpu.roll(vt, (ATTN_KV_HEADS - j) * HEAD_DIM, 0)
        vt_ref[j] = jnp.where(row < HEAD_DIM, vj, jnp.where(row == HEAD_DIM, 1.0, 0.0)).astype(BF16)


def _in_proj(x2, norm1, w_in, qk_gain, cos_t, sin_t, seg_qk, n, tm):
    t = x2.shape[0]
    nblk = n // tm
    return pl.pallas_call(
        _in_proj_kernel,
        grid=(t // tm,),
        in_specs=[
            pl.BlockSpec((tm, D_MODEL), lambda i: (i, 0)),
            _full((1, D_MODEL)),
            _full((D_MODEL, IN_COLS)),
            _full((1, QK_COLS)),
            pl.BlockSpec((tm, LANES), lambda i: (i % nblk, 0)),
            pl.BlockSpec((tm, LANES), lambda i: (i % nblk, 0)),
            _full((QK_COLS, QK_COLS)),
        ],
        out_specs=[
            pl.BlockSpec((tm, RWKV_COLS), lambda i: (i, 0)),
            pl.BlockSpec((tm, ATTN_WIDTH), lambda i: (i, 0)),
            pl.BlockSpec((ATTN_KV_HEADS, tm, HEAD_DIM), lambda i: (0, i, 0)),
            pl.BlockSpec((ATTN_KV_HEADS, LANES, tm), lambda i: (0, 0, i)),
        ],
        out_shape=[
            jax.ShapeDtypeStruct((t, RWKV_COLS), F32),
            jax.ShapeDtypeStruct((t, ATTN_WIDTH), BF16),
            jax.ShapeDtypeStruct((ATTN_KV_HEADS, t, HEAD_DIM), BF16),
            jax.ShapeDtypeStruct((ATTN_KV_HEADS, LANES, t), BF16),
        ],
        compiler_params=_cparams(("parallel",)),
        name="in_proj",
    )(x2, norm1, w_in, qk_gain, cos_t, sin_t, seg_qk)


def _attn_kernel(q_ref, k_ref, vt_ref, o_ref, qt_ref, m_ref, acc_ref, *, tq, tk, kb):
    ki = pl.program_id(2)
    nsub = KV_GROUP * tq // QSUB

    @pl.when(ki == 0)
    def _():
        qt = q_ref[...].astype(F32).T
        for j in range(ATTN_KV_HEADS):
            for g in range(KV_GROUP):
                h = j * KV_GROUP + g
                qt_ref[j, :, g * tq:(g + 1) * tq] = qt[h * HEAD_DIM:(h + 1) * HEAD_DIM, :].astype(BF16)
        m_ref[...] = jnp.full(m_ref.shape, -jnp.inf, F32)
        acc_ref[...] = jnp.zeros(acc_ref.shape, F32)

    tiles = [(j, s * QSUB) for j in range(ATTN_KV_HEADS) for s in range(nsub)]
    scores, probs, alphas = {}, {}, {}
    for i in range(len(tiles) + 2):
        if i < len(tiles):
            j, q0 = tiles[i]
            scores[i] = _dot(k_ref[j], qt_ref[j, :, q0:q0 + QSUB])
        if 0 <= i - 1 < len(tiles):
            j, q0 = tiles[i - 1]
            st = scores.pop(i - 1)
            m = m_ref[j, :, q0:q0 + QSUB]
            m_new = jnp.maximum(m, jnp.max(st, axis=0, keepdims=True))
            probs[i - 1] = jnp.exp2(st - m_new).astype(BF16)
            alphas[i - 1] = jnp.exp2(m - m_new)
            m_ref[j, :, q0:q0 + QSUB] = m_new
        if 0 <= i - 2 < len(tiles):
            j, q0 = tiles[i - 2]
            acc_ref[j, :, q0:q0 + QSUB] = (alphas.pop(i - 2) * acc_ref[j, :, q0:q0 + QSUB]
                                           + _dot(vt_ref[j, 0:VROWS, :], probs.pop(i - 2)))

    @pl.when(ki == pl.num_programs(2) - 1)
    def _():
        outs = []
        for j in range(ATTN_KV_HEADS):
            for g in range(KV_GROUP):
                a = acc_ref[j, :, g * tq:(g + 1) * tq]
                outs.append(a[:HEAD_DIM, :] / a[HEAD_DIM:HEAD_DIM + 1, :])
        o_ref[...] = jnp.concatenate(outs, axis=0).T.astype(o_ref.dtype)


def _attention(q, k, vt, b, n, tq, tk):
    t = b * n
    nq, nk = n // tq, n // tk
    nqry = KV_GROUP * tq
    kb = min(tk, KEY_CHUNK)
    return pl.pallas_call(
        functools.partial(_attn_kernel, tq=tq, tk=tk, kb=kb),
        grid=(b, nq, nk),
        in_specs=[
            pl.BlockSpec((tq, ATTN_WIDTH), lambda bi, qi, ki: (bi * nq + qi, 0)),
            pl.BlockSpec((ATTN_KV_HEADS, tk, HEAD_DIM), lambda bi, qi, ki: (0, bi * nk + ki, 0)),
            pl.BlockSpec((ATTN_KV_HEADS, LANES, tk), lambda bi, qi, ki: (0, 0, bi * nk + ki)),
        ],
        out_specs=pl.BlockSpec((tq, ATTN_WIDTH), lambda bi, qi, ki: (bi * nq + qi, 0)),
        out_shape=jax.ShapeDtypeStruct((t, ATTN_WIDTH), BF16),
        scratch_shapes=[
            pltpu.VMEM((ATTN_KV_HEADS, HEAD_DIM, nqry), BF16),
            pltpu.VMEM((ATTN_KV_HEADS, 1, nqry), F32),
            pltpu.VMEM((ATTN_KV_HEADS, VROWS, nqry), F32),
        ],
        compiler_params=_cparams(("parallel", "parallel", "arbitrary")),
        name="attention",
    )(q, k, vt)


def _shift_feat(p, prev_row, next_row, taps):
    c = p.shape[0]
    rows = lax.broadcasted_iota(I32, p.shape, 0)
    prev = jnp.where(rows == 0, prev_row, pltpu.roll(p, 1, 0))
    nxt = jnp.where(rows == c - 1, next_row, pltpu.roll(p, c - 1, 0))
    return taps[0:1] * prev + taps[1:2] * p + taps[2:3] * nxt


def _split_feat(feat):
    r = feat[:, :RWKV_WIDTH]
    k = feat[:, RWKV_WIDTH:2 * RWKV_WIDTH]
    v = feat[:, 2 * RWKV_WIDTH:3 * RWKV_WIDTH]
    o = 3 * RWKV_WIDTH
    wd = feat[:, o:o + W_LORA]
    ad = feat[:, o + W_LORA:o + W_LORA + A_LORA]
    gd = feat[:, o + W_LORA + A_LORA:]
    return r, k, v, wd, ad, gd


def _in_context_rate(ad_b, a0_row, a_up):
    return _sigmoid(a0_row + _dot(ad_b, a_up))


def _halo_rows(hp_ref, hn_ref, is_first, is_last):
    prev_row = jnp.where(is_first, 0.0, hp_ref[7:8, :])
    next_row = jnp.where(is_last, 0.0, hn_ref[0:1, :])
    return prev_row, next_row


def _masked_stack(x):
    lane = lax.broadcasted_iota(I32, x.shape, 1)
    return jnp.concatenate([jnp.where(lane < HEAD_DIM, x, 0.0), jnp.where(lane >= HEAD_DIM, x, 0.0)], axis=0)


def _wkv_chunks(ops, s_ref):
    c = CHUNK
    ti = lax.broadcasted_iota(I32, (c, c), 0)
    si = lax.broadcasted_iota(I32, (c, c), 1)
    rr = lax.broadcasted_iota(I32, (PAIR, PAIR), 0)
    cc = lax.broadcasted_iota(I32, (PAIR, PAIR), 1)
    eye = (rr == cc).astype(F32)
    rr = rr & (c - 1)
    cc = cc & (c - 1)

    scaled = []
    for (r, kd, v, an, bb, lw, rev) in ops:
        l_incl = ((si >= ti) if rev else (si <= ti)).astype(BF16)
        lw_hi, lw_mid, lw_lo = _split3(lw)
        cum = _dot(l_incl, lw_hi) + _dot(l_incl, lw_mid) + _dot(l_incl, lw_lo)
        tot = jnp.sum(lw, axis=0, keepdims=True)
        e_neg = jnp.exp(-cum)
        e_tot = jnp.exp(tot - cum)
        scaled.append(dict(a=an * jnp.exp(cum - lw), r=r * jnp.exp(cum), b=bb * e_neg, k=kd * e_neg,
                           bh=bb * e_tot, kh=kd * e_tot, v=v, g=jnp.exp(tot), rev=rev))

    chains = []
    for d, sc in enumerate(scaled):
        tri_incl = (cc >= rr) if sc["rev"] else (cc <= rr)
        tri_strict = (cc > rr) if sc["rev"] else (cc < rr)
        for p in range(N_PAIRS):
            sl = slice(p * PAIR, (p + 1) * PAIR)
            ch = dict(d=d, p=p, sl=sl, tri_incl=tri_incl, tri_strict=tri_strict, g=sc["g"][:, sl])
            for name in ("a", "b", "k", "bh", "kh", "v"):
                ch[name] = _masked_stack(sc[name][:, sl]).astype(BF16)
            ch["r"] = _masked_stack(sc["r"][:, sl])
            chains.append(ch)

    for ch in chains:
        gram = _dot_nt(jnp.concatenate([ch["a"], ch["r"].astype(BF16)], axis=0),
                       jnp.concatenate([ch["b"], ch["k"]], axis=0))
        ch["x"] = jnp.where(ch["tri_strict"], gram[:PAIR, :PAIR], 0.0)
        ch["m_ak"] = jnp.where(ch["tri_strict"], gram[:PAIR, PAIR:], 0.0).astype(BF16)
        ch["m_rb"] = jnp.where(ch["tri_incl"], gram[PAIR:, :PAIR], 0.0).astype(BF16)
        ch["m_rk"] = jnp.where(ch["tri_incl"], gram[PAIR:, PAIR:], 0.0).astype(BF16)
        ch["t_inv"] = eye + ch["x"]

    for _ in range(int(math.log2(c)) - 1):
        for ch in chains:
            xb = ch["x"].astype(BF16)
            ch["x"] = _dot(xb, xb)
        for ch in chains:
            ch["t_inv"] = ch["t_inv"] + _dot(ch["t_inv"].astype(BF16), ch["x"].astype(BF16))

    for ch in chains:
        ch["mv"] = _dot(ch["m_ak"], ch["v"]).astype(BF16)
        ch["vk"] = _dot(ch["v"].astype(F32).T.astype(BF16), ch["kh"])
        ch["s_in"] = s_ref[ch["d"], ch["p"]]
    for ch in chains:
        ch["ta"] = _dot(ch["t_inv"].astype(BF16), jnp.concatenate([ch["a"], ch["mv"]], axis=1))
    for ch in chains:
        ch["gh"] = _dot(ch["ta"].T.astype(BF16), ch["bh"])
        ch["mr"] = _dot(ch["m_rb"], ch["ta"].astype(BF16))
        ch["y_in"] = _dot(ch["m_rk"], ch["v"])
    ys = [[], []]
    for ch in chains:
        s_b = ch["s_in"].astype(BF16)
        q_h = (ch["r"] + ch["mr"][:, :PAIR]).astype(BF16)
        y_ms = _dot_nt(q_h, s_b) + ch["mr"][:, PAIR:] + ch["y_in"]
        ys[ch["d"]].append(y_ms[:c] + y_ms[c:])
        gh = ch["gh"]
        s_ref[ch["d"], ch["p"]] = ch["s_in"] * ch["g"] + _dot(s_b, gh[:PAIR].astype(BF16)) + gh[PAIR:] + ch["vk"]
    return [jnp.concatenate(y, axis=1) for y in ys]


def _wkv_kernel(pf_ref, hpf_ref, hnf_ref, pr_ref, hpr_ref, hnr_ref,
                taps_ref, w0_ref, wup_ref, a0_ref, aup_ref, kk_ref, ka_ref, seg_ref,
                yf_ref, yr_ref, s_ref):
    c = pl.program_id(1)
    nc = pl.num_programs(1)

    @pl.when(c == 0)
    def _():
        s_ref[...] = jnp.zeros(s_ref.shape, F32)

    taps = taps_ref[...]
    seg = seg_ref[...]
    refs = ((pf_ref, hpf_ref, hnf_ref), (pr_ref, hpr_ref, hnr_ref))
    chunks = (c, nc - 1 - c)
    dirs = (0, 1)
    feats = []
    for d in dirs:
        p_ref, hp_ref, hn_ref = refs[d]
        prev_row, next_row = _halo_rows(hp_ref, hn_ref, chunks[d] == 0, chunks[d] == nc - 1)
        feats.append(_split_feat(_shift_feat(p_ref[...], prev_row, next_row, taps)))
    kks = [f[1] * kk_ref[...] for f in feats]
    norms = [_seg_sum(kk * kk, seg) for kk in kks]
    wlin = [_dot(jnp.tanh(feats[d][3]).astype(BF16), wup_ref[d]) for d in dirs]
    alin = [_dot(feats[d][4].astype(BF16), aup_ref[d]) for d in dirs]
    ops = []
    for d in dirs:
        r, k, v = feats[d][:3]
        kk = kks[d] * lax.rsqrt(jnp.maximum(norms[d], KK_EPS_SQ))
        lw = -DECAY_SCALE * _sigmoid(w0_ref[d:d + 1, :] + wlin[d])
        a = _sigmoid(a0_ref[d:d + 1, :] + alin[d])
        kd = k * (1.0 + (a - 1.0) * ka_ref[...])
        ops.append((r, kd, v, -kk, kk * a, lw, d == 1))
    y_f, y_r = _wkv_chunks(ops, s_ref)
    yf_ref[...] = y_f
    yr_ref[...] = y_r


def _wkv(feat, taps, w0, w_up, a0, a_up, k_k, k_a, seg_r, b, n):
    t = b * n
    nc = n // CHUNK
    c8 = CHUNK // 8
    last8 = t // 8 - 1

    def fwd(bi, ci):
        return bi * nc + ci

    def rev(bi, ci):
        return bi * nc + (nc - 1 - ci)

    def specs(chunk_of):
        return [
            pl.BlockSpec((CHUNK, RWKV_COLS), lambda bi, ci: (chunk_of(bi, ci), 0)),
            pl.BlockSpec((8, RWKV_COLS), lambda bi, ci: (jnp.maximum(chunk_of(bi, ci) * c8 - 1, 0), 0)),
            pl.BlockSpec((8, RWKV_COLS), lambda bi, ci: (jnp.minimum((chunk_of(bi, ci) + 1) * c8, last8), 0)),
        ]

    return pl.pallas_call(
        _wkv_kernel,
        grid=(b, nc),
        in_specs=specs(fwd) + specs(rev) + [
            _full((3, RWKV_COLS)),
            _full((2, RWKV_WIDTH)),
            _full((2, W_LORA, RWKV_WIDTH)),
            _full((2, RWKV_WIDTH)),
            _full((2, A_LORA, RWKV_WIDTH)),
            _full((1, RWKV_WIDTH)),
            _full((1, RWKV_WIDTH)),
            _full((RWKV_WIDTH, RWKV_WIDTH)),
        ],
        out_specs=[
            pl.BlockSpec((CHUNK, RWKV_WIDTH), lambda bi, ci: (fwd(bi, ci), 0)),
            pl.BlockSpec((CHUNK, RWKV_WIDTH), lambda bi, ci: (rev(bi, ci), 0)),
        ],
        out_shape=[jax.ShapeDtypeStruct((t, RWKV_WIDTH), F32)] * 2,
        scratch_shapes=[pltpu.VMEM((2, N_PAIRS, PAIR, PAIR), F32)],
        compiler_params=_cparams(("parallel", "arbitrary")),
        name="wkv",
    )(feat, feat, feat, feat, feat, feat, taps, w0, w_up, a0, a_up, k_k, k_a, seg_r)


def _rwkv_post_kernel(p_ref, hp_ref, hn_ref, yf_ref, yr_ref, taps_ref, a0_ref, aup_ref, ka_ref, rk_ref,
                      gup_ref, lng_ref, lnb_ref, seg_ref, o_ref, *, blocks_per_seq):
    i = pl.program_id(0)
    pos = i % blocks_per_seq
    prev_row, next_row = _halo_rows(hp_ref, hn_ref, pos == 0, pos == blocks_per_seq - 1)
    feat = _shift_feat(p_ref[...], prev_row, next_row, taps_ref[...])
    r, k, v, _, ad, gd = _split_feat(feat)
    seg = seg_ref[...]
    ad_b = ad.astype(BF16)
    a_sum = (_in_context_rate(ad_b, a0_ref[0:1, :], aup_ref[0])
             + _in_context_rate(ad_b, a0_ref[1:2, :], aup_ref[1]))
    k_bonus = k * (1.0 + (0.5 * a_sum - 1.0) * ka_ref[...])
    gate = _dot(_sigmoid(gd).astype(BF16), gup_ref[...])
    y = yf_ref[...] + yr_ref[...]
    mu = _seg_sum(y, seg) * (1.0 / HEAD_DIM)
    yc = y - mu
    var = _seg_sum(yc * yc, seg) * (1.0 / HEAD_DIM)
    yn = yc * lax.rsqrt(var + GN_EPS) * lng_ref[...] + lnb_ref[...]
    bonus = _seg_sum(r * k_bonus * rk_ref[...], seg) * v
    o_ref[...] = ((yn + bonus) * gate).astype(o_ref.dtype)


def _rwkv_post(feat, y_f, y_r, taps, a0, a_up, k_a, r_k, g_up, ln_g, ln_b, seg_r, n, tm):
    t = feat.shape[0]
    bps = n // tm
    t8 = tm // 8
    last8 = t // 8 - 1
    row = lambda i: (i, 0)
    return pl.pallas_call(
        functools.partial(_rwkv_post_kernel, blocks_per_seq=bps),
        grid=(t // tm,),
        in_specs=[
            pl.BlockSpec((tm, RWKV_COLS), row),
            pl.BlockSpec((8, RWKV_COLS), lambda i: (jnp.maximum(i * t8 - 1, 0), 0)),
            pl.BlockSpec((8, RWKV_COLS), lambda i: (jnp.minimum((i + 1) * t8, last8), 0)),
            pl.BlockSpec((tm, RWKV_WIDTH), row),
            pl.BlockSpec((tm, RWKV_WIDTH), row),
            _full((3, RWKV_COLS)),
            _full((2, RWKV_WIDTH)),
            _full((2, A_LORA, RWKV_WIDTH)),
            _full((1, RWKV_WIDTH)),
            _full((1, RWKV_WIDTH)),
            _full((G_LORA, RWKV_WIDTH)),
            _full((1, RWKV_WIDTH)),
            _full((1, RWKV_WIDTH)),
            _full((RWKV_WIDTH, RWKV_WIDTH)),
        ],
        out_specs=pl.BlockSpec((tm, RWKV_WIDTH), row),
        out_shape=jax.ShapeDtypeStruct((t, RWKV_WIDTH), BF16),
        compiler_params=_cparams(("parallel",)),
        name="rwkv_post",
    )(feat, feat, feat, y_f, y_r, taps, a0, a_up, k_a, r_k, g_up, ln_g, ln_b, seg_r)


def _out_proj_kernel(x_ref, ro_ref, ao_ref, wo_ref, g2_ref, wrh_ref, wrl_ref, h_ref, xn_ref, aff_ref):
    mix = _dot(ro_ref[...], wo_ref[:RWKV_WIDTH, :]) + _dot(ao_ref[...], wo_ref[RWKV_WIDTH:, :])
    h = x_ref[...] + mix
    h_ref[...] = h
    ms = jnp.mean(h * h, axis=-1, keepdims=True)
    xn = h * lax.rsqrt(ms + NORM_EPS) * g2_ref[...]
    xn_ref[...] = xn
    xh, xl = _split2(xn)
    wrh = wrh_ref[...]
    logits = _dot(xh, wrh) + _dot(xl, wrh) + _dot(xh, wrl_ref[...])
    lane = lax.broadcasted_iota(I32, logits.shape, 1)
    logits = jnp.where(lane < N_EXPERTS, logits, -jnp.inf)
    e = jnp.exp(logits - jnp.max(logits, axis=-1, keepdims=True))
    aff = e / jnp.sum(e, axis=-1, keepdims=True)
    aff_ref[...] = aff.T[:N_EXPERTS, :]


def _out_proj(x2, rwkv_out, attn_out, w_out, norm2, wr_hi, wr_lo, tm):
    t = x2.shape[0]
    row = lambda i: (i, 0)
    return pl.pallas_call(
        _out_proj_kernel,
        grid=(t // tm,),
        in_specs=[
            pl.BlockSpec((tm, D_MODEL), row),
            pl.BlockSpec((tm, RWKV_WIDTH), row),
            pl.BlockSpec((tm, ATTN_WIDTH), row),
            _full((D_MODEL, D_MODEL)),
            _full((1, D_MODEL)),
            _full((D_MODEL, LANES)),
            _full((D_MODEL, LANES)),
        ],
        out_specs=[
            pl.BlockSpec((tm, D_MODEL), row),
            pl.BlockSpec((tm, D_MODEL), row),
            pl.BlockSpec((N_EXPERTS, tm), lambda i: (0, i)),
        ],
        out_shape=[
            jax.ShapeDtypeStruct((t, D_MODEL), F32),
            jax.ShapeDtypeStruct((t, D_MODEL), F32),
            jax.ShapeDtypeStruct((N_EXPERTS, t), F32),
        ],
        compiler_params=_cparams(("parallel",)),
        name="out_proj",
    )(x2, rwkv_out, attn_out, w_out, norm2, wr_hi, wr_lo)


def _topk_kernel(aff_ref, inv_ref, rb_ref, *, cap, t, tb):
    nrows = t // LANES
    nblk = t // tb
    rows_per_blk = tb // LANES

    def bits(sl):
        return pltpu.bitcast(aff_ref[:, sl], I32)

    def bisect(i, prefix):
        cand = prefix | jnp.left_shift(jnp.int32(1), 30 - i)
        cnt = jnp.sum((bits(slice(None)) >= cand).astype(I32), axis=1, keepdims=True)
        return jnp.where(cnt >= cap, cand, prefix)

    tau = lax.fori_loop(0, 31, bisect, jnp.zeros((N_EXPERTS, 1), I32))
    n_gt = jnp.sum((bits(slice(None)) > tau).astype(I32), axis=1, keepdims=True)
    need = (cap - n_gt).astype(F32)

    upper = (lax.broadcasted_iota(I32, (tb, tb), 0) < lax.broadcasted_iota(I32, (tb, tb), 1)).astype(BF16)
    tok_row = lax.broadcasted_iota(I32, (tb, nrows), 0) >> 7
    out_row = lax.broadcasted_iota(I32, (tb, nrows), 1)

    def block(b, carry):
        c_eq, c_pos, row_cnt = carry
        start = pl.multiple_of(b * tb, tb)
        xs = bits(pl.ds(start, tb))
        gt = xs > tau
        eq = xs == tau
        eq_b = eq.astype(BF16)
        rank_eq = c_eq + _dot(eq_b, upper)
        sel = gt | (eq & (rank_eq < need))
        sel_b = sel.astype(BF16)
        pos = c_pos + _dot(sel_b, upper)
        inv_ref[:, pl.ds(start, tb)] = jnp.where(sel, pos.astype(I32), -1)
        place = (out_row == tok_row + b * rows_per_blk).astype(BF16)
        row_cnt = row_cnt + _dot(sel_b, place)
        c_eq = c_eq + jnp.sum(eq_b.astype(F32), axis=1, keepdims=True)
        c_pos = c_pos + jnp.sum(sel_b.astype(F32), axis=1, keepdims=True)
        return c_eq, c_pos, row_cnt

    zero = jnp.zeros((N_EXPERTS, 1), F32)
    _, _, row_cnt = lax.fori_loop(0, nblk, block, (zero, zero, jnp.zeros((N_EXPERTS, nrows), F32)))
    upper_r = (lax.broadcasted_iota(I32, (nrows, nrows), 0) < lax.broadcasted_iota(I32, (nrows, nrows), 1)).astype(BF16)
    rb_ref[...] = _dot(row_cnt.astype(BF16), upper_r).astype(I32)


def _topk(aff_t, cap):
    t = aff_t.shape[1]
    tb = min(512, t)
    return pl.pallas_call(
        functools.partial(_topk_kernel, cap=cap, t=t, tb=tb),
        in_specs=[_full((N_EXPERTS, t))],
        out_specs=[_full((N_EXPERTS, t)), _full((N_EXPERTS, t // LANES))],
        out_shape=[jax.ShapeDtypeStruct((N_EXPERTS, t), I32), jax.ShapeDtypeStruct((N_EXPERTS, t // LANES), I32)],
        compiler_params=pltpu.CompilerParams(vmem_limit_bytes=VMEM_LIMIT),
        name="topk",
    )(aff_t)


def _compact_kernel(rb_ref, inv_ref, idx_ref, acc_ref, *, nrows, ntiles):
    e = pl.program_id(0)
    acc_ref[...] = jnp.zeros(acc_ref.shape, F32)
    sub = lax.broadcasted_iota(I32, (8, LANES), 0)
    lane_tok = lax.broadcasted_iota(I32, (8, LANES), 1)
    slot_off = lax.broadcasted_iota(I32, (LANES, LANES), 0)

    def body(r, _):
        start = pl.multiple_of(r * LANES, LANES)
        inv_row = inv_ref[0, :, pl.ds(start, LANES)]
        tok = lane_tok + r * LANES
        lmat = jnp.where(sub == 0, (tok >> 8).astype(F32),
                         jnp.where(sub == 1, (tok & 255).astype(F32), 0.0)).astype(BF16)
        q0 = rb_ref[e * nrows + r] >> 7
        inv_b = jnp.broadcast_to(inv_row, (LANES, LANES))
        for dq in range(2):
            onehot_t = (inv_b == slot_off + (q0 + dq) * LANES).astype(BF16)
            acc_ref[q0 + dq] += _dot_nt(lmat, onehot_t)
        return 0

    lax.fori_loop(0, nrows, body, 0, unroll=4)
    idx_ref[0] = (acc_ref[0:ntiles, 0, :] * 256.0 + acc_ref[0:ntiles, 1, :]).astype(I32)


def _compact(rb_flat, inv, cap):
    t = inv.shape[1]
    nrows = t // LANES
    ntiles = cap // LANES
    return pl.pallas_call(
        functools.partial(_compact_kernel, nrows=nrows, ntiles=ntiles),
        grid_spec=pltpu.PrefetchScalarGridSpec(
            num_scalar_prefetch=1,
            grid=(N_EXPERTS,),
            in_specs=[pl.BlockSpec((1, 1, t), lambda e, rb: (e, 0, 0))],
            out_specs=pl.BlockSpec((1, ntiles, LANES), lambda e, rb: (e, 0, 0)),
            scratch_shapes=[pltpu.VMEM((ntiles + 2, 8, LANES), F32)],
        ),
        out_shape=jax.ShapeDtypeStruct((N_EXPERTS, ntiles, LANES), I32),
        compiler_params=_cparams(("arbitrary",)),
        name="compact",
    )(rb_flat, inv.reshape(N_EXPERTS, 1, t))


def _expert_kernel(idx_ref, idxn_ref, x_hbm, wg_ref, wu_ref, wd_ref, o_ref, xbuf, sems, *, ts, nsteps):
    g = pl.program_id(0) * pl.num_programs(1) + pl.program_id(1)
    slot = g % 2

    def row_copy(ids, j, sl):
        return pltpu.make_async_copy(x_hbm.at[pl.ds(ids[0, 0, j], 1)], xbuf.at[sl, pl.ds(j, 1)], sems.at[sl])

    @pl.when(g == 0)
    def _():
        for j in range(ts):
            row_copy(idx_ref, j, 0).start()

    for j in range(ts):
        row_copy(idx_ref, j, slot).wait()
    quarter = ts // 4
    for j in range(0, quarter):
        row_copy(idxn_ref, j, 1 - slot).start()
    xs = xbuf[slot].astype(BF16)
    gp = _dot(xs, wg_ref[0])
    for j in range(quarter, 2 * quarter):
        row_copy(idxn_ref, j, 1 - slot).start()
    up = _dot(xs, wu_ref[0])
    for j in range(2 * quarter, 3 * quarter):
        row_copy(idxn_ref, j, 1 - slot).start()
    hid = (gp * _sigmoid(gp) * up).astype(BF16)
    o_ref[...] = _dot(hid, wd_ref[0]).astype(o_ref.dtype)
    for j in range(3 * quarter, ts):
        row_copy(idxn_ref, j, 1 - slot).start()

    @pl.when(g == nsteps - 1)
    def _():
        for j in range(ts):
            row_copy(idxn_ref, j, 1 - slot).wait()


def _experts(idx, xn, w_gate, w_up, w_down, cap, ts):
    nt = cap // ts
    nsteps = N_EXPERTS * nt
    idx3 = idx.reshape(nsteps, 1, ts)
    wspec = lambda a, b: pl.BlockSpec((1, a, b), lambda e, i: (e, 0, 0))
    return pl.pallas_call(
        functools.partial(_expert_kernel, ts=ts, nsteps=nsteps),
        grid=(N_EXPERTS, nt),
        in_specs=[
            pl.BlockSpec((1, 1, ts), lambda e, i: (e * nt + i, 0, 0), memory_space=pltpu.SMEM),
            pl.BlockSpec((1, 1, ts), lambda e, i: (jnp.minimum(e * nt + i + 1, nsteps - 1), 0, 0),
                         memory_space=pltpu.SMEM),
            pl.BlockSpec(memory_space=pl.ANY),
            wspec(D_MODEL, D_EXPERT),
            wspec(D_MODEL, D_EXPERT),
            wspec(D_EXPERT, D_MODEL),
        ],
        out_specs=pl.BlockSpec((ts, D_MODEL), lambda e, i: (e * nt + i, 0)),
        out_shape=jax.ShapeDtypeStruct((N_EXPERTS * cap, D_MODEL), BF16),
        scratch_shapes=[
            pltpu.VMEM((2, ts, D_MODEL), F32),
            pltpu.SemaphoreType.DMA((2,)),
        ],
        compiler_params=_cparams(("arbitrary", "arbitrary")),
        name="experts",
    )(idx3, idx3, xn, w_gate, w_up, w_down)


def _combine_kernel(rb_ref, h_ref, inv_ref, aff_ref, o_hbm, y_ref, cbuf, sems, *, nrows, cap):
    r = pl.program_id(0)
    slot = r % 2

    def chunk_copy(rr, e, sl):
        p0 = jnp.minimum(rb_ref[e * nrows + rr] & ~(CHUNK_ALIGN - 1), cap - CHUNK_ROWS)
        return pltpu.make_async_copy(o_hbm.at[e, pl.ds(pl.multiple_of(p0, CHUNK_ALIGN), CHUNK_ROWS), :],
                                     cbuf.at[sl, e], sems.at[sl]), p0

    @pl.when(r == 0)
    def _():
        for e in range(N_EXPERTS):
            chunk_copy(0, e, 0)[0].start()

    nxt = jnp.minimum(r + 1, nrows - 1)
    for e in range(N_EXPERTS):
        chunk_copy(nxt, e, 1 - slot)[0].start()

    pad = jnp.zeros((LANES - N_EXPERTS, LANES), F32)
    inv_t = jnp.concatenate([inv_ref[...].astype(F32), pad], axis=0).T
    aff_t = jnp.concatenate([aff_ref[...], pad], axis=0).T
    lane = lax.broadcasted_iota(I32, (LANES, CHUNK_ROWS), 1)
    acc = h_ref[...]
    bases = []
    for e in range(N_EXPERTS):
        cp, p0 = chunk_copy(r, e, slot)
        cp.wait()
        bases.append(p0)
    for e in range(N_EXPERTS):
        onehot = inv_t[:, e:e + 1] == (lane + bases[e]).astype(F32)
        gsel = jnp.where(onehot, aff_t[:, e:e + 1], 0.0).astype(BF16)
        acc = acc + _dot(gsel, cbuf[slot, e])
    y_ref[...] = acc

    @pl.when(r == nrows - 1)
    def _():
        for e in range(N_EXPERTS):
            chunk_copy(nxt, e, 1 - slot)[0].wait()


def _combine(rb_flat, h, inv, aff_t, eout, cap):
    t = h.shape[0]
    nrows = t // LANES
    return pl.pallas_call(
        functools.partial(_combine_kernel, nrows=nrows, cap=cap),
        grid_spec=pltpu.PrefetchScalarGridSpec(
            num_scalar_prefetch=1,
            grid=(nrows,),
            in_specs=[
                pl.BlockSpec((LANES, D_MODEL), lambda r, rb: (r, 0)),
                pl.BlockSpec((N_EXPERTS, LANES), lambda r, rb: (0, r)),
                pl.BlockSpec((N_EXPERTS, LANES), lambda r, rb: (0, r)),
                pl.BlockSpec(memory_space=pl.ANY),
            ],
            out_specs=pl.BlockSpec((LANES, D_MODEL), lambda r, rb: (r, 0)),
            scratch_shapes=[
                pltpu.VMEM((2, N_EXPERTS, CHUNK_ROWS, D_MODEL), BF16),
                pltpu.SemaphoreType.DMA((2,)),
            ],
        ),
        out_shape=jax.ShapeDtypeStruct((t, D_MODEL), F32),
        compiler_params=_cparams(("arbitrary",)),
        name="combine",
    )(rb_flat, h, inv, aff_t, eout.reshape(N_EXPERTS, cap, D_MODEL))


def _rope_tables(n):
    pos = jnp.arange(n, dtype=F32)
    row = jnp.floor(pos / GRID_W)
    col = pos - row * GRID_W
    inv_freq = ROPE_THETA ** (-jnp.arange(ROPE_PAIRS, dtype=F32) / ROPE_PAIRS)
    ang_r = row[:, None] * inv_freq
    ang_c = col[:, None] * inv_freq
    cos = jnp.concatenate([jnp.cos(ang_r), jnp.cos(ang_r), jnp.cos(ang_c), jnp.cos(ang_c)], axis=1)
    sin = jnp.concatenate([-jnp.sin(ang_r), jnp.sin(ang_r), -jnp.sin(ang_c), jnp.sin(ang_c)], axis=1)
    reps = LANES // HEAD_DIM
    return jnp.tile(cos, (1, reps)), jnp.tile(sin, (1, reps))


def _block_diag_ones(width):
    i = jnp.arange(width) // HEAD_DIM
    return (i[:, None] == i[None, :]).astype(BF16)


def _pick(n, prefs):
    for p in prefs:
        if n % p == 0:
            return p
    raise ValueError(f"no block size in {prefs} divides {n}")


def _layer(x, prm):
    b, n, _ = x.shape
    t = b * n
    cap = CAPACITY_FACTOR * t // N_EXPERTS
    assert n % CHUNK == 0 and n % GRID_W == 0 and t % LANES == 0 and cap % LANES == 0
    x2 = x.reshape(t, D_MODEL)
    cos_t, sin_t = _rope_tables(n)

    tm = _pick(n, (512, 256, 128, 64))
    feat, q, k, vt = _in_proj(x2, prm["norm1"], prm["w_in"], prm["qk_gain"], cos_t, sin_t, prm["seg_qk"], n, tm)

    tq = _pick(n, (512, 256, 128))
    tk = _pick(n, (1024, 512, 256, 128))
    attn_out = _attention(q, k, vt, b, n, tq, tk)

    y_f, y_r = _wkv(feat, prm["taps"], prm["w0"], prm["w_up"], prm["a0"], prm["a_up"], prm["k_k"], prm["k_a"],
                    prm["seg_r"], b, n)
    tp = _pick(n, (256, 128, 64))
    rwkv_out = _rwkv_post(feat, y_f, y_r, prm["taps"], prm["a0"], prm["a_up"], prm["k_a"], prm["r_k"],
                          prm["g_up"], prm["ln_g"], prm["ln_b"], prm["seg_r"], n, tp)

    h, xn, aff_t = _out_proj(x2, rwkv_out, attn_out, prm["w_out"], prm["norm2"], prm["wr_hi"], prm["wr_lo"], tm)

    inv, rb = _topk(aff_t, cap)
    rb_flat = rb.reshape(-1)
    idx = _compact(rb_flat, inv, cap)
    ts = _pick(cap, (256, 128))
    eout = _experts(idx.reshape(N_EXPERTS, cap), xn, prm["w_gate"], prm["w_up_e"], prm["w_down"], cap, ts)
    y = _combine(rb_flat, h, inv, aff_t, eout, cap)
    return y.reshape(b, n, D_MODEL)


def _prepare(norm1, w_in, shift_taps, w0, w_lora_up, a0, a_lora_up, k_k, k_a, r_k, g_lora_up, ln_g, ln_b,
             q_norm, k_norm, w_out, norm2, w_router, expert_w_gate, expert_w_up, expert_w_down):
    l = 0
    wr = jnp.pad(w_router[l], ((0, 0), (0, LANES - N_EXPERTS)))
    wr_hi = wr.astype(BF16)
    wr_lo = (wr - wr_hi.astype(F32)).astype(BF16)
    qk_gain = jnp.concatenate([jnp.tile(q_norm[l], ATTN_Q_HEADS), jnp.tile(k_norm[l], ATTN_KV_HEADS)])
    return dict(
        norm1=norm1[l].reshape(1, D_MODEL),
        w_in=w_in[l].astype(BF16),
        taps=shift_taps[l],
        w0=w0[l],
        w_up=w_lora_up[l].astype(BF16),
        a0=a0[l],
        a_up=a_lora_up[l].astype(BF16),
        k_k=k_k[l].reshape(1, RWKV_WIDTH),
        k_a=k_a[l].reshape(1, RWKV_WIDTH),
        r_k=r_k[l].reshape(1, RWKV_WIDTH),
        g_up=g_lora_up[l].astype(BF16),
        ln_g=ln_g[l].reshape(1, RWKV_WIDTH),
        ln_b=ln_b[l].reshape(1, RWKV_WIDTH),
        qk_gain=qk_gain.reshape(1, QK_COLS),
        w_out=w_out[l].astype(BF16),
        norm2=norm2[l].reshape(1, D_MODEL),
        wr_hi=wr_hi,
        wr_lo=wr_lo,
        w_gate=expert_w_gate[l].astype(BF16),
        w_up_e=expert_w_up[l].astype(BF16),
        w_down=expert_w_down[l].astype(BF16),
        seg_qk=_block_diag_ones(QK_COLS),
        seg_r=_block_diag_ones(RWKV_WIDTH),
    )


def kernel(x_prompt, x_sample, norm1, w_in, shift_taps, w0, w_lora_up, a0, a_lora_up, k_k, k_a, r_k, g_lora_up,
           ln_g, ln_b, q_norm, k_norm, w_out, norm2, w_router, expert_w_gate, expert_w_up, expert_w_down):
    assert norm1.shape[0] == 1, "single-layer trunk"
    prm = _prepare(norm1, w_in, shift_taps, w0, w_lora_up, a0, a_lora_up, k_k, k_a, r_k, g_lora_up, ln_g, ln_b,
                   q_norm, k_norm, w_out, norm2, w_router, expert_w_gate, expert_w_up, expert_w_down)
    return (_layer(x_prompt, prm), _layer(x_sample, prm))
```

```python
import functools
import math

import jax
import jax.numpy as jnp
from jax import lax
from jax.experimental import pallas as pl
from jax.experimental.pallas import tpu as pltpu

F32 = jnp.float32
BF16 = jnp.bfloat16
I32 = jnp.int32

D_MODEL = 1024
HEAD_DIM = 64
RWKV_WIDTH = 512
RWKV_HEADS = RWKV_WIDTH // HEAD_DIM
ATTN_WIDTH = D_MODEL - RWKV_WIDTH
ATTN_Q_HEADS = ATTN_WIDTH // HEAD_DIM
ATTN_KV_HEADS = 2
KV_GROUP = ATTN_Q_HEADS // ATTN_KV_HEADS
KV_WIDTH = ATTN_KV_HEADS * HEAD_DIM
W_LORA = 64
A_LORA = 64
G_LORA = 128
RWKV_COLS = 3 * RWKV_WIDTH + W_LORA + A_LORA + G_LORA
IN_COLS = RWKV_COLS + ATTN_WIDTH + 2 * KV_WIDTH
QK_COLS = ATTN_WIDTH + KV_WIDTH
GRID_W = 64
ROPE_THETA = 10000.0
ROPE_PAIRS = HEAD_DIM // 4
N_EXPERTS = 16
CAPACITY_FACTOR = 2
D_EXPERT = 1024
NORM_EPS = 1e-6
GN_EPS = 64e-5
KK_EPS_SQ = 1e-24
DECAY_SCALE = math.exp(-0.5)
Q_SCALE = math.log2(math.e) / math.sqrt(HEAD_DIM)

LANES = 128
CHUNK = 64
PAIR = 2 * HEAD_DIM
N_PAIRS = RWKV_WIDTH // PAIR
WKV_CHUNKS_PER_STEP = 4
QSUB = 256
KEY_CHUNK = 1024
VROWS = 80
ATTN_PIPE_WIDTH = 2
CHUNK_ALIGN = 16
CHUNK_ROWS = LANES + CHUNK_ALIGN
VMEM_LIMIT = 48 * 1024 * 1024


def _cparams(sem):
    return pltpu.CompilerParams(dimension_semantics=sem, vmem_limit_bytes=VMEM_LIMIT)


def _full(shape):
    nd = len(shape)
    return pl.BlockSpec(shape, lambda *_: (0,) * nd)


def _dot(a, b):
    return jnp.dot(a, b, preferred_element_type=F32)


def _dot_nt(a, b):
    return lax.dot_general(a, b, (((1,), (1,)), ((), ())), preferred_element_type=F32)


def _split2(x):
    hi = x.astype(BF16)
    lo = (x - hi.astype(F32)).astype(BF16)
    return hi, lo


def _split3(x):
    hi = x.astype(BF16)
    r1 = x - hi.astype(F32)
    mid = r1.astype(BF16)
    lo = (r1 - mid.astype(F32)).astype(BF16)
    return hi, mid, lo


def _seg_sum(x, seg):
    hi, lo = _split2(x)
    return _dot(hi, seg) + _dot(lo, seg)


def _sigmoid(x):
    return 1.0 / (1.0 + jnp.exp(-x))


def _in_proj_kernel(x_ref, g_ref, w_ref, qkg_ref, cos_ref, sin_ref, seg_ref,
                    feat_ref, q_ref, k_ref, vt_ref):
    x = x_ref[...]
    ms = jnp.mean(x * x, axis=-1, keepdims=True)
    xn = (x * lax.rsqrt(ms + NORM_EPS) * g_ref[...]).astype(BF16)
    p = _dot(xn, w_ref[...])
    feat_ref[...] = p[:, :RWKV_COLS]
    qk = p[:, RWKV_COLS:RWKV_COLS + QK_COLS]
    ssq = _seg_sum(qk * qk, seg_ref[...])
    qkn = qk * lax.rsqrt(ssq * (1.0 / HEAD_DIM) + NORM_EPS) * qkg_ref[...]
    reps = QK_COLS // LANES
    cos = jnp.concatenate([cos_ref[...]] * reps, axis=1)
    sin = jnp.concatenate([sin_ref[...]] * reps, axis=1)
    lane = lax.broadcasted_iota(I32, qkn.shape, 1)
    first = (lane & (2 * ROPE_PAIRS - 1)) < ROPE_PAIRS
    partner = jnp.where(first, pltpu.roll(qkn, QK_COLS - ROPE_PAIRS, 1), pltpu.roll(qkn, ROPE_PAIRS, 1))
    rot = qkn * cos + partner * sin
    q_ref[...] = (rot[:, :ATTN_WIDTH] * Q_SCALE).astype(BF16)
    kr = rot[:, ATTN_WIDTH:].astype(BF16)
    vt = p[:, RWKV_COLS + QK_COLS:].T
    row = lax.broadcasted_iota(I32, vt.shape, 0)
    for j in range(ATTN_KV_HEADS):
        k_ref[j] = kr[:, j * HEAD_DIM:(j + 1) * HEAD_DIM]
        vj = vt if j == 0 else pltpu.roll(vt, (ATTN_KV_HEADS - j) * HEAD_DIM, 0)
        vt_ref[j] = jnp.where(row < HEAD_DIM, vj, jnp.where(row == HEAD_DIM, 1.0, 0.0)).astype(BF16)


def _in_proj(x2, norm1, w_in, qk_gain, cos_t, sin_t, seg_qk, n, tm):
    t = x2.shape[0]
    nblk = n // tm
    return pl.pallas_call(
        _in_proj_kernel,
        grid=(t // tm,),
        in_specs=[
            pl.BlockSpec((tm, D_MODEL), lambda i: (i, 0)),
            _full((1, D_MODEL)),
            _full((D_MODEL, IN_COLS)),
            _full((1, QK_COLS)),
            pl.BlockSpec((tm, LANES), lambda i: (i % nblk, 0)),
            pl.BlockSpec((tm, LANES), lambda i: (i % nblk, 0)),
            _full((QK_COLS, QK_COLS)),
        ],
        out_specs=[
            pl.BlockSpec((tm, RWKV_COLS), lambda i: (i, 0)),
            pl.BlockSpec((tm, ATTN_WIDTH), lambda i: (i, 0)),
            pl.BlockSpec((ATTN_KV_HEADS, tm, HEAD_DIM), lambda i: (0, i, 0)),
            pl.BlockSpec((ATTN_KV_HEADS, LANES, tm), lambda i: (0, 0, i)),
        ],
        out_shape=[
            jax.ShapeDtypeStruct((t, RWKV_COLS), F32),
            jax.ShapeDtypeStruct((t, ATTN_WIDTH), BF16),
            jax.ShapeDtypeStruct((ATTN_KV_HEADS, t, HEAD_DIM), BF16),
            jax.ShapeDtypeStruct((ATTN_KV_HEADS, LANES, t), BF16),
        ],
        compiler_params=_cparams(("parallel",)),
        name="in_proj",
    )(x2, norm1, w_in, qk_gain, cos_t, sin_t, seg_qk)


def _attn_kernel(q_ref, k_ref, vt_ref, o_ref, qt_ref, m_ref, acc_ref, *, tq, tk, kb):
    ki = pl.program_id(2)
    nsub = KV_GROUP * tq // QSUB

    @pl.when(ki == 0)
    def _():
        qt = q_ref[...].astype(F32).T
        for j in range(ATTN_KV_HEADS):
            for g in range(KV_GROUP):
                h = j * KV_GROUP + g
                qt_ref[j, :, g * tq:(g + 1) * tq] = qt[h * HEAD_DIM:(h + 1) * HEAD_DIM, :].astype(BF16)
        m_ref[...] = jnp.full(m_ref.shape, -jnp.inf, F32)
        acc_ref[...] = jnp.zeros(acc_ref.shape, F32)

    tiles = [(j, s * QSUB, c * kb) for c in range(tk // kb) for j in range(ATTN_KV_HEADS) for s in range(nsub)]
    scores, probs, alphas = {}, {}, {}
    width = ATTN_PIPE_WIDTH
    ngroups = len(tiles) // width

    def group(gidx):
        return range(gidx * width, (gidx + 1) * width) if 0 <= gidx < ngroups else ()

    for step in range(ngroups + 2):
        for i in group(step):
            j, q0, k0 = tiles[i]
            scores[i] = _dot(k_ref[j, k0:k0 + kb, :], qt_ref[j, :, q0:q0 + QSUB])
        for i in group(step - 1):
            j, q0, k0 = tiles[i]
            st = scores.pop(i)
            m = m_ref[j, :, q0:q0 + QSUB]
            m_new = jnp.maximum(m, jnp.max(st, axis=0, keepdims=True))
            probs[i] = jnp.exp2(st - m_new).astype(BF16)
            alphas[i] = jnp.exp2(m - m_new)
            m_ref[j, :, q0:q0 + QSUB] = m_new
        for i in group(step - 2):
            j, q0, k0 = tiles[i]
            acc_ref[j, :, q0:q0 + QSUB] = (alphas.pop(i) * acc_ref[j, :, q0:q0 + QSUB]
                                           + _dot(vt_ref[j, 0:VROWS, k0:k0 + kb], probs.pop(i)))

    @pl.when(ki == pl.num_programs(2) - 1)
    def _():
        outs = []
        for j in range(ATTN_KV_HEADS):
            for g in range(KV_GROUP):
                a = acc_ref[j, :, g * tq:(g + 1) * tq]
                outs.append(a[:HEAD_DIM, :] / a[HEAD_DIM:HEAD_DIM + 1, :])
        o_ref[...] = jnp.concatenate(outs, axis=0).T.astype(o_ref.dtype)


def _attention(q, k, vt, b, n, tq, tk):
    t = b * n
    nq, nk = n // tq, n // tk
    nqry = KV_GROUP * tq
    kb = min(tk, KEY_CHUNK)
    return pl.pallas_call(
        functools.partial(_attn_kernel, tq=tq, tk=tk, kb=kb),
        grid=(b, nq, nk),
        in_specs=[
            pl.BlockSpec((tq, ATTN_WIDTH), lambda bi, qi, ki: (bi * nq + qi, 0)),
            pl.BlockSpec((ATTN_KV_HEADS, tk, HEAD_DIM), lambda bi, qi, ki: (0, bi * nk + ki, 0)),
            pl.BlockSpec((ATTN_KV_HEADS, LANES, tk), lambda bi, qi, ki: (0, 0, bi * nk + ki)),
        ],
        out_specs=pl.BlockSpec((tq, ATTN_WIDTH), lambda bi, qi, ki: (bi * nq + qi, 0)),
        out_shape=jax.ShapeDtypeStruct((t, ATTN_WIDTH), BF16),
        scratch_shapes=[
            pltpu.VMEM((ATTN_KV_HEADS, HEAD_DIM, nqry), BF16),
            pltpu.VMEM((ATTN_KV_HEADS, 1, nqry), F32),
            pltpu.VMEM((ATTN_KV_HEADS, VROWS, nqry), F32),
        ],
        compiler_params=_cparams(("parallel", "parallel", "arbitrary")),
        name="attention",
    )(q, k, vt)


def _shift_feat(p, prev_row, next_row, taps):
    c = p.shape[0]
    rows = lax.broadcasted_iota(I32, p.shape, 0)
    prev = jnp.where(rows == 0, prev_row, pltpu.roll(p, 1, 0))
    nxt = jnp.where(rows == c - 1, next_row, pltpu.roll(p, c - 1, 0))
    return taps[0:1] * prev + taps[1:2] * p + taps[2:3] * nxt


def _split_feat(feat):
    r = feat[:, :RWKV_WIDTH]
    k = feat[:, RWKV_WIDTH:2 * RWKV_WIDTH]
    v = feat[:, 2 * RWKV_WIDTH:3 * RWKV_WIDTH]
    o = 3 * RWKV_WIDTH
    wd = feat[:, o:o + W_LORA]
    ad = feat[:, o + W_LORA:o + W_LORA + A_LORA]
    gd = feat[:, o + W_LORA + A_LORA:]
    return r, k, v, wd, ad, gd


def _in_context_rate(ad_b, a0_row, a_up):
    return _sigmoid(a0_row + _dot(ad_b, a_up))


def _halo_rows(hp_ref, hn_ref, is_first, is_last):
    prev_row = jnp.where(is_first, 0.0, hp_ref[7:8, :])
    next_row = jnp.where(is_last, 0.0, hn_ref[0:1, :])
    return prev_row, next_row


def _masked_stack(x):
    lane = lax.broadcasted_iota(I32, x.shape, 1)
    return jnp.concatenate([jnp.where(lane < HEAD_DIM, x, 0.0), jnp.where(lane >= HEAD_DIM, x, 0.0)], axis=0)


def _wkv_chunks(ops, s_ref):
    c = CHUNK
    ti = lax.broadcasted_iota(I32, (c, c), 0)
    si = lax.broadcasted_iota(I32, (c, c), 1)
    rr = lax.broadcasted_iota(I32, (PAIR, PAIR), 0)
    cc = lax.broadcasted_iota(I32, (PAIR, PAIR), 1)
    eye = (rr == cc).astype(F32)
    rr = rr & (c - 1)
    cc = cc & (c - 1)

    chains = []
    for d, seq, (r, kd, v, an, bb, lw) in ops:
        rev = d == 1
        l_incl = ((si >= ti) if rev else (si <= ti)).astype(BF16)
        lw_hi, lw_mid, lw_lo = _split3(lw)
        cum = _dot(l_incl, lw_hi) + _dot(l_incl, lw_mid) + _dot(l_incl, lw_lo)
        tot = jnp.sum(lw, axis=0, keepdims=True)
        e_neg = jnp.exp(-cum)
        e_tot = jnp.exp(tot - cum)
        sc = dict(a=an * jnp.exp(cum - lw), r=r * jnp.exp(cum), b=bb * e_neg, k=kd * e_neg,
                  bh=bb * e_tot, kh=kd * e_tot, v=v)
        g = jnp.exp(tot)
        tri_incl = (cc >= rr) if rev else (cc <= rr)
        tri_strict = (cc > rr) if rev else (cc < rr)
        for p in range(N_PAIRS):
            sl = slice(p * PAIR, (p + 1) * PAIR)
            ch = dict(d=d, seq=seq, p=p, tri_incl=tri_incl, tri_strict=tri_strict, g=g[:, sl])
            for name in ("a", "b", "k", "bh", "kh", "v"):
                ch[name] = _masked_stack(sc[name][:, sl]).astype(BF16)
            ch["r"] = _masked_stack(sc["r"][:, sl])
            chains.append(ch)

    for ch in chains:
        gram = _dot_nt(jnp.concatenate([ch["a"], ch["r"].astype(BF16)], axis=0),
                       jnp.concatenate([ch["b"], ch["k"]], axis=0))
        ch["x"] = jnp.where(ch["tri_strict"], gram[:PAIR, :PAIR], 0.0)
        ch["m_ak"] = jnp.where(ch["tri_strict"], gram[:PAIR, PAIR:], 0.0).astype(BF16)
        ch["m_rb"] = jnp.where(ch["tri_incl"], gram[PAIR:, :PAIR], 0.0).astype(BF16)
        ch["m_rk"] = jnp.where(ch["tri_incl"], gram[PAIR:, PAIR:], 0.0).astype(BF16)
        ch["t_inv"] = eye

    nsteps = int(math.log2(c))
    for step in range(nsteps):
        last = step == nsteps - 1
        for ch in chains:
            yb = ch["x"].astype(BF16)
            tb = ch["t_inv"].astype(BF16)
            z = _dot(yb, tb if last else jnp.concatenate([yb, tb], axis=1))
            if last:
                ch["t_inv"] = ch["t_inv"] + z
            else:
                ch["x"] = z[:, :PAIR]
                ch["t_inv"] = ch["t_inv"] + z[:, PAIR:]

    for ch in chains:
        ch["mv"] = _dot(ch["m_ak"], ch["v"]).astype(BF16)
        ch["vk"] = _dot(ch["v"].astype(F32).T.astype(BF16), ch["kh"])
    for ch in chains:
        ch["ta"] = _dot(ch["t_inv"].astype(BF16), jnp.concatenate([ch["a"], ch["mv"]], axis=1))
    for ch in chains:
        ch["gh"] = _dot(ch["ta"].T.astype(BF16), ch["bh"])
        ch["mr"] = _dot(ch["m_rb"], ch["ta"].astype(BF16))
        ch["y_in"] = _dot(ch["m_rk"], ch["v"])
    state = {}
    parts = {}
    for ch in sorted(chains, key=lambda ch: ch["seq"]):
        key = (ch["d"], ch["p"])
        s_in = state[key] if key in state else s_ref[ch["d"], ch["p"]]
        s_b = s_in.astype(BF16)
        q_h = (ch["r"] + ch["mr"][:, :PAIR]).astype(BF16)
        y_ms = _dot_nt(q_h, s_b) + ch["mr"][:, PAIR:] + ch["y_in"]
        parts.setdefault((ch["d"], ch["seq"]), []).append(y_ms[:c] + y_ms[c:])
        gh = ch["gh"]
        state[key] = s_in * ch["g"] + _dot(s_b, gh[:PAIR].astype(BF16)) + gh[PAIR:] + ch["vk"]
    for (d, p), s_out in state.items():
        s_ref[d, p] = s_out
    return {key: jnp.concatenate(y, axis=1) for key, y in parts.items()}


def _wkv_kernel(pf_ref, hpf_ref, hnf_ref, pr_ref, hpr_ref, hnr_ref,
                taps_ref, w0_ref, wup_ref, a0_ref, aup_ref, kk_ref, ka_ref, seg_ref,
                yf_ref, yr_ref, s_ref, *, cps):
    c = pl.program_id(1)
    nc = pl.num_programs(1)

    @pl.when(c == 0)
    def _():
        s_ref[...] = jnp.zeros(s_ref.shape, F32)

    taps = taps_ref[...]
    seg = seg_ref[...]
    refs = ((pf_ref, hpf_ref, hnf_ref), (pr_ref, hpr_ref, hnr_ref))
    blocks = (c, nc - 1 - c)
    dirs = (0, 1)
    feats = []
    for d in dirs:
        p_ref, hp_ref, hn_ref = refs[d]
        prev_row, next_row = _halo_rows(hp_ref, hn_ref, blocks[d] == 0, blocks[d] == nc - 1)
        feats.append(_split_feat(_shift_feat(p_ref[...], prev_row, next_row, taps)))
    kks = [f[1] * kk_ref[...] for f in feats]
    norms = [_seg_sum(kk * kk, seg) for kk in kks]
    wlin = [_dot(jnp.tanh(feats[d][3]).astype(BF16), wup_ref[d]) for d in dirs]
    alin = [_dot(feats[d][4].astype(BF16), aup_ref[d]) for d in dirs]
    ops = []
    for d in dirs:
        r, k, v = feats[d][:3]
        kk = kks[d] * lax.rsqrt(jnp.maximum(norms[d], KK_EPS_SQ))
        lw = -DECAY_SCALE * _sigmoid(w0_ref[d:d + 1, :] + wlin[d])
        a = _sigmoid(a0_ref[d:d + 1, :] + alin[d])
        kd = k * (1.0 + (a - 1.0) * ka_ref[...])
        full = (r, kd, v, -kk, kk * a, lw)
        for seq in range(cps):
            ci = seq if d == 0 else cps - 1 - seq
            ops.append((d, seq, tuple(x[ci * CHUNK:(ci + 1) * CHUNK] for x in full)))
    ys = _wkv_chunks(ops, s_ref)
    for d, y_ref in enumerate((yf_ref, yr_ref)):
        for seq in range(cps):
            ci = seq if d == 0 else cps - 1 - seq
            y_ref[ci * CHUNK:(ci + 1) * CHUNK, :] = ys[(d, seq)]


def _wkv(feat, taps, w0, w_up, a0, a_up, k_k, k_a, seg_r, b, n):
    t = b * n
    cps = WKV_CHUNKS_PER_STEP
    rows = cps * CHUNK
    nc = n // rows
    c8 = rows // 8
    last8 = t // 8 - 1

    def fwd(bi, ci):
        return bi * nc + ci

    def rev(bi, ci):
        return bi * nc + (nc - 1 - ci)

    def specs(block_of):
        return [
            pl.BlockSpec((rows, RWKV_COLS), lambda bi, ci: (block_of(bi, ci), 0)),
            pl.BlockSpec((8, RWKV_COLS), lambda bi, ci: (jnp.maximum(block_of(bi, ci) * c8 - 1, 0), 0)),
            pl.BlockSpec((8, RWKV_COLS), lambda bi, ci: (jnp.minimum((block_of(bi, ci) + 1) * c8, last8), 0)),
        ]

    return pl.pallas_call(
        functools.partial(_wkv_kernel, cps=cps),
        grid=(b, nc),
        in_specs=specs(fwd) + specs(rev) + [
            _full((3, RWKV_COLS)),
            _full((2, RWKV_WIDTH)),
            _full((2, W_LORA, RWKV_WIDTH)),
            _full((2, RWKV_WIDTH)),
            _full((2, A_LORA, RWKV_WIDTH)),
            _full((1, RWKV_WIDTH)),
            _full((1, RWKV_WIDTH)),
            _full((RWKV_WIDTH, RWKV_WIDTH)),
        ],
        out_specs=[
            pl.BlockSpec((rows, RWKV_WIDTH), lambda bi, ci: (fwd(bi, ci), 0)),
            pl.BlockSpec((rows, RWKV_WIDTH), lambda bi, ci: (rev(bi, ci), 0)),
        ],
        out_shape=[jax.ShapeDtypeStruct((t, RWKV_WIDTH), F32)] * 2,
        scratch_shapes=[pltpu.VMEM((2, N_PAIRS, PAIR, PAIR), F32)],
        compiler_params=_cparams(("parallel", "arbitrary")),
        name="wkv",
    )(feat, feat, feat, feat, feat, feat, taps, w0, w_up, a0, a_up, k_k, k_a, seg_r)


def _rwkv_post_kernel(p_ref, hp_ref, hn_ref, yf_ref, yr_ref, taps_ref, a0_ref, aup_ref, ka_ref, rk_ref,
                      gup_ref, lng_ref, lnb_ref, seg_ref, o_ref, *, blocks_per_seq):
    i = pl.program_id(0)
    pos = i % blocks_per_seq
    prev_row, next_row = _halo_rows(hp_ref, hn_ref, pos == 0, pos == blocks_per_seq - 1)
    feat = _shift_feat(p_ref[...], prev_row, next_row, taps_ref[...])
    r, k, v, _, ad, gd = _split_feat(feat)
    seg = seg_ref[...]
    ad_b = ad.astype(BF16)
    a_sum = (_in_context_rate(ad_b, a0_ref[0:1, :], aup_ref[0])
             + _in_context_rate(ad_b, a0_ref[1:2, :], aup_ref[1]))
    k_bonus = k * (1.0 + (0.5 * a_sum - 1.0) * ka_ref[...])
    gate = _dot(_sigmoid(gd).astype(BF16), gup_ref[...])
    y = yf_ref[...] + yr_ref[...]
    mu = _seg_sum(y, seg) * (1.0 / HEAD_DIM)
    yc = y - mu
    var = _seg_sum(yc * yc, seg) * (1.0 / HEAD_DIM)
    yn = yc * lax.rsqrt(var + GN_EPS) * lng_ref[...] + lnb_ref[...]
    bonus = _seg_sum(r * k_bonus * rk_ref[...], seg) * v
    o_ref[...] = ((yn + bonus) * gate).astype(o_ref.dtype)


def _rwkv_post(feat, y_f, y_r, taps, a0, a_up, k_a, r_k, g_up, ln_g, ln_b, seg_r, n, tm):
    t = feat.shape[0]
    bps = n // tm
    t8 = tm // 8
    last8 = t // 8 - 1
    row = lambda i: (i, 0)
    return pl.pallas_call(
        functools.partial(_rwkv_post_kernel, blocks_per_seq=bps),
        grid=(t // tm,),
        in_specs=[
            pl.BlockSpec((tm, RWKV_COLS), row),
            pl.BlockSpec((8, RWKV_COLS), lambda i: (jnp.maximum(i * t8 - 1, 0), 0)),
            pl.BlockSpec((8, RWKV_COLS), lambda i: (jnp.minimum((i + 1) * t8, last8), 0)),
            pl.BlockSpec((tm, RWKV_WIDTH), row),
            pl.BlockSpec((tm, RWKV_WIDTH), row),
            _full((3, RWKV_COLS)),
            _full((2, RWKV_WIDTH)),
            _full((2, A_LORA, RWKV_WIDTH)),
            _full((1, RWKV_WIDTH)),
            _full((1, RWKV_WIDTH)),
            _full((G_LORA, RWKV_WIDTH)),
            _full((1, RWKV_WIDTH)),
            _full((1, RWKV_WIDTH)),
            _full((RWKV_WIDTH, RWKV_WIDTH)),
        ],
        out_specs=pl.BlockSpec((tm, RWKV_WIDTH), row),
        out_shape=jax.ShapeDtypeStruct((t, RWKV_WIDTH), BF16),
        compiler_params=_cparams(("parallel",)),
        name="rwkv_post",
    )(feat, feat, feat, y_f, y_r, taps, a0, a_up, k_a, r_k, g_up, ln_g, ln_b, seg_r)


def _out_proj_kernel(x_ref, ro_ref, ao_ref, wo_ref, g2_ref, wrh_ref, wrl_ref, h_ref, xn_ref, aff_ref):
    mix = _dot(ro_ref[...], wo_ref[:RWKV_WIDTH, :]) + _dot(ao_ref[...], wo_ref[RWKV_WIDTH:, :])
    h = x_ref[...] + mix
    h_ref[...] = h
    ms = jnp.mean(h * h, axis=-1, keepdims=True)
    xn = h * lax.rsqrt(ms + NORM_EPS) * g2_ref[...]
    xn_ref[...] = xn
    xh, xl = _split2(xn)
    wrh = wrh_ref[...]
    logits = _dot(xh, wrh) + _dot(xl, wrh) + _dot(xh, wrl_ref[...])
    lane = lax.broadcasted_iota(I32, logits.shape, 1)
    logits = jnp.where(lane < N_EXPERTS, logits, -jnp.inf)
    e = jnp.exp(logits - jnp.max(logits, axis=-1, keepdims=True))
    aff = e / jnp.sum(e, axis=-1, keepdims=True)
    aff_ref[...] = aff.T[:N_EXPERTS, :]


def _out_proj(x2, rwkv_out, attn_out, w_out, norm2, wr_hi, wr_lo, tm):
    t = x2.shape[0]
    row = lambda i: (i, 0)
    return pl.pallas_call(
        _out_proj_kernel,
        grid=(t // tm,),
        in_specs=[
            pl.BlockSpec((tm, D_MODEL), row),
            pl.BlockSpec((tm, RWKV_WIDTH), row),
            pl.BlockSpec((tm, ATTN_WIDTH), row),
            _full((D_MODEL, D_MODEL)),
            _full((1, D_MODEL)),
            _full((D_MODEL, LANES)),
            _full((D_MODEL, LANES)),
        ],
        out_specs=[
            pl.BlockSpec((tm, D_MODEL), row),
            pl.BlockSpec((tm, D_MODEL), row),
            pl.BlockSpec((N_EXPERTS, tm), lambda i: (0, i)),
        ],
        out_shape=[
            jax.ShapeDtypeStruct((t, D_MODEL), F32),
            jax.ShapeDtypeStruct((t, D_MODEL), F32),
            jax.ShapeDtypeStruct((N_EXPERTS, t), F32),
        ],
        compiler_params=_cparams(("parallel",)),
        name="out_proj",
    )(x2, rwkv_out, attn_out, w_out, norm2, wr_hi, wr_lo)


def _topk_kernel(aff_ref, inv_ref, rb_ref, *, cap, t, tb):
    nrows = t // LANES
    nblk = t // tb
    rows_per_blk = tb // LANES

    def bits(sl):
        return pltpu.bitcast(aff_ref[:, sl], I32)

    def bisect(i, prefix):
        cand = prefix | jnp.left_shift(jnp.int32(1), 30 - i)
        cnt = jnp.sum((bits(slice(None)) >= cand).astype(I32), axis=1, keepdims=True)
        return jnp.where(cnt >= cap, cand, prefix)

    tau = lax.fori_loop(0, 31, bisect, jnp.zeros((N_EXPERTS, 1), I32))
    n_gt = jnp.sum((bits(slice(None)) > tau).astype(I32), axis=1, keepdims=True)
    need = (cap - n_gt).astype(F32)

    upper = (lax.broadcasted_iota(I32, (tb, tb), 0) < lax.broadcasted_iota(I32, (tb, tb), 1)).astype(BF16)
    tok_row = lax.broadcasted_iota(I32, (tb, nrows), 0) >> 7
    out_row = lax.broadcasted_iota(I32, (tb, nrows), 1)

    def block(b, carry):
        c_eq, c_pos, row_cnt = carry
        start = pl.multiple_of(b * tb, tb)
        xs = bits(pl.ds(start, tb))
        gt = xs > tau
        eq = xs == tau
        eq_b = eq.astype(BF16)
        rank_eq = c_eq + _dot(eq_b, upper)
        sel = gt | (eq & (rank_eq < need))
        sel_b = sel.astype(BF16)
        pos = c_pos + _dot(sel_b, upper)
        inv_ref[:, pl.ds(start, tb)] = jnp.where(sel, pos.astype(I32), -1)
        place = (out_row == tok_row + b * rows_per_blk).astype(BF16)
        row_cnt = row_cnt + _dot(sel_b, place)
        c_eq = c_eq + jnp.sum(eq_b.astype(F32), axis=1, keepdims=True)
        c_pos = c_pos + jnp.sum(sel_b.astype(F32), axis=1, keepdims=True)
        return c_eq, c_pos, row_cnt

    zero = jnp.zeros((N_EXPERTS, 1), F32)
    _, _, row_cnt = lax.fori_loop(0, nblk, block, (zero, zero, jnp.zeros((N_EXPERTS, nrows), F32)))
    upper_r = (lax.broadcasted_iota(I32, (nrows, nrows), 0) < lax.broadcasted_iota(I32, (nrows, nrows), 1)).astype(BF16)
    rb_ref[...] = _dot(row_cnt.astype(BF16), upper_r).astype(I32)


def _topk(aff_t, cap):
    t = aff_t.shape[1]
    tb = min(512, t)
    return pl.pallas_call(
        functools.partial(_topk_kernel, cap=cap, t=t, tb=tb),
        in_specs=[_full((N_EXPERTS, t))],
        out_specs=[_full((N_EXPERTS, t)), _full((N_EXPERTS, t // LANES))],
        out_shape=[jax.ShapeDtypeStruct((N_EXPERTS, t), I32), jax.ShapeDtypeStruct((N_EXPERTS, t // LANES), I32)],
        compiler_params=pltpu.CompilerParams(vmem_limit_bytes=VMEM_LIMIT),
        name="topk",
    )(aff_t)


def _compact_kernel(rb_ref, inv_ref, idx_ref, acc_ref, *, nrows, ntiles):
    e = pl.program_id(0)
    acc_ref[...] = jnp.zeros(acc_ref.shape, F32)
    sub = lax.broadcasted_iota(I32, (8, LANES), 0)
    lane_tok = lax.broadcasted_iota(I32, (8, LANES), 1)
    slot_off = lax.broadcasted_iota(I32, (LANES, LANES), 0)

    def body(r, _):
        start = pl.multiple_of(r * LANES, LANES)
        inv_row = inv_ref[0, :, pl.ds(start, LANES)]
        tok = lane_tok + r * LANES
        lmat = jnp.where(sub == 0, (tok >> 8).astype(F32),
                         jnp.where(sub == 1, (tok & 255).astype(F32), 0.0)).astype(BF16)
        q0 = rb_ref[e * nrows + r] >> 7
        inv_b = jnp.broadcast_to(inv_row, (LANES, LANES))
        for dq in range(2):
            onehot_t = (inv_b == slot_off + (q0 + dq) * LANES).astype(BF16)
            acc_ref[q0 + dq] += _dot_nt(lmat, onehot_t)
        return 0

    lax.fori_loop(0, nrows, body, 0, unroll=4)
    idx_ref[0] = (acc_ref[0:ntiles, 0, :] * 256.0 + acc_ref[0:ntiles, 1, :]).astype(I32)


def _compact(rb_flat, inv, cap):
    t = inv.shape[1]
    nrows = t // LANES
    ntiles = cap // LANES
    return pl.pallas_call(
        functools.partial(_compact_kernel, nrows=nrows, ntiles=ntiles),
        grid_spec=pltpu.PrefetchScalarGridSpec(
            num_scalar_prefetch=1,
            grid=(N_EXPERTS,),
            in_specs=[pl.BlockSpec((1, 1, t), lambda e, rb: (e, 0, 0))],
            out_specs=pl.BlockSpec((1, ntiles, LANES), lambda e, rb: (e, 0, 0)),
            scratch_shapes=[pltpu.VMEM((ntiles + 2, 8, LANES), F32)],
        ),
        out_shape=jax.ShapeDtypeStruct((N_EXPERTS, ntiles, LANES), I32),
        compiler_params=_cparams(("arbitrary",)),
        name="compact",
    )(rb_flat, inv.reshape(N_EXPERTS, 1, t))


def _expert_kernel(idx_ref, idxn_ref, x_hbm, wg_ref, wu_ref, wd_ref, o_ref, xbuf, sems, *, ts, nsteps):
    g = pl.program_id(0) * pl.num_programs(1) + pl.program_id(1)
    slot = g % 2

    def row_copy(ids, j, sl):
        return pltpu.make_async_copy(x_hbm.at[pl.ds(ids[0, 0, j], 1)], xbuf.at[sl, pl.ds(j, 1)], sems.at[sl])

    @pl.when(g == 0)
    def _():
        for j in range(ts):
            row_copy(idx_ref, j, 0).start()

    for j in range(ts):
        row_copy(idx_ref, j, slot).wait()
    quarter = ts // 4
    for j in range(0, quarter):
        row_copy(idxn_ref, j, 1 - slot).start()
    xs = xbuf[slot].astype(BF16)
    gp = _dot(xs, wg_ref[0])
    for j in range(quarter, 2 * quarter):
        row_copy(idxn_ref, j, 1 - slot).start()
    up = _dot(xs, wu_ref[0])
    for j in range(2 * quarter, 3 * quarter):
        row_copy(idxn_ref, j, 1 - slot).start()
    hid = (gp * _sigmoid(gp) * up).astype(BF16)
    o_ref[...] = _dot(hid, wd_ref[0]).astype(o_ref.dtype)
    for j in range(3 * quarter, ts):
        row_copy(idxn_ref, j, 1 - slot).start()

    @pl.when(g == nsteps - 1)
    def _():
        for j in range(ts):
            row_copy(idxn_ref, j, 1 - slot).wait()


def _experts(idx, xn, w_gate, w_up, w_down, cap, ts):
    nt = cap // ts
    nsteps = N_EXPERTS * nt
    idx3 = idx.reshape(nsteps, 1, ts)
    wspec = lambda a, b: pl.BlockSpec((1, a, b), lambda e, i: (e, 0, 0))
    return pl.pallas_call(
        functools.partial(_expert_kernel, ts=ts, nsteps=nsteps),
        grid=(N_EXPERTS, nt),
        in_specs=[
            pl.BlockSpec((1, 1, ts), lambda e, i: (e * nt + i, 0, 0), memory_space=pltpu.SMEM),
            pl.BlockSpec((1, 1, ts), lambda e, i: (jnp.minimum(e * nt + i + 1, nsteps - 1), 0, 0),
                         memory_space=pltpu.SMEM),
            pl.BlockSpec(memory_space=pl.ANY),
            wspec(D_MODEL, D_EXPERT),
            wspec(D_MODEL, D_EXPERT),
            wspec(D_EXPERT, D_MODEL),
        ],
        out_specs=pl.BlockSpec((ts, D_MODEL), lambda e, i: (e * nt + i, 0)),
        out_shape=jax.ShapeDtypeStruct((N_EXPERTS * cap, D_MODEL), BF16),
        scratch_shapes=[
            pltpu.VMEM((2, ts, D_MODEL), F32),
            pltpu.SemaphoreType.DMA((2,)),
        ],
        compiler_params=_cparams(("arbitrary", "arbitrary")),
        name="experts",
    )(idx3, idx3, xn, w_gate, w_up, w_down)


def _combine_kernel(rb_ref, h_ref, inv_ref, aff_ref, o_hbm, y_ref, cbuf, sems, *, nrows, cap):
    r = pl.program_id(0)
    slot = r % 2

    def chunk_copy(rr, e, sl):
        p0 = jnp.minimum(rb_ref[e * nrows + rr] & ~(CHUNK_ALIGN - 1), cap - CHUNK_ROWS)
        return pltpu.make_async_copy(o_hbm.at[e, pl.ds(pl.multiple_of(p0, CHUNK_ALIGN), CHUNK_ROWS), :],
                                     cbuf.at[sl, e], sems.at[sl]), p0

    @pl.when(r == 0)
    def _():
        for e in range(N_EXPERTS):
            chunk_copy(0, e, 0)[0].start()

    nxt = jnp.minimum(r + 1, nrows - 1)
    for e in range(N_EXPERTS):
        chunk_copy(nxt, e, 1 - slot)[0].start()

    pad = jnp.zeros((LANES - N_EXPERTS, LANES), F32)
    inv_t = jnp.concatenate([inv_ref[...].astype(F32), pad], axis=0).T
    aff_t = jnp.concatenate([aff_ref[...], pad], axis=0).T
    lane = lax.broadcasted_iota(I32, (LANES, CHUNK_ROWS), 1)
    acc = h_ref[...]
    bases = []
    for e in range(N_EXPERTS):
        cp, p0 = chunk_copy(r, e, slot)
        cp.wait()
        bases.append(p0)
    for e in range(N_EXPERTS):
        onehot = inv_t[:, e:e + 1] == (lane + bases[e]).astype(F32)
        gsel = jnp.where(onehot, aff_t[:, e:e + 1], 0.0).astype(BF16)
        acc = acc + _dot(gsel, cbuf[slot, e])
    y_ref[...] = acc

    @pl.when(r == nrows - 1)
    def _():
        for e in range(N_EXPERTS):
            chunk_copy(nxt, e, 1 - slot)[0].wait()


def _combine(rb_flat, h, inv, aff_t, eout, cap):
    t = h.shape[0]
    nrows = t // LANES
    return pl.pallas_call(
        functools.partial(_combine_kernel, nrows=nrows, cap=cap),
        grid_spec=pltpu.PrefetchScalarGridSpec(
            num_scalar_prefetch=1,
            grid=(nrows,),
            in_specs=[
                pl.BlockSpec((LANES, D_MODEL), lambda r, rb: (r, 0)),
                pl.BlockSpec((N_EXPERTS, LANES), lambda r, rb: (0, r)),
                pl.BlockSpec((N_EXPERTS, LANES), lambda r, rb: (0, r)),
                pl.BlockSpec(memory_space=pl.ANY),
            ],
            out_specs=pl.BlockSpec((LANES, D_MODEL), lambda r, rb: (r, 0)),
            scratch_shapes=[
                pltpu.VMEM((2, N_EXPERTS, CHUNK_ROWS, D_MODEL), BF16),
                pltpu.SemaphoreType.DMA((2,)),
            ],
        ),
        out_shape=jax.ShapeDtypeStruct((t, D_MODEL), F32),
        compiler_params=_cparams(("arbitrary",)),
        name="combine",
    )(rb_flat, h, inv, aff_t, eout.reshape(N_EXPERTS, cap, D_MODEL))


def _rope_tables(n):
    pos = jnp.arange(n, dtype=F32)
    row = jnp.floor(pos / GRID_W)
    col = pos - row * GRID_W
    inv_freq = ROPE_THETA ** (-jnp.arange(ROPE_PAIRS, dtype=F32) / ROPE_PAIRS)
    ang_r = row[:, None] * inv_freq
    ang_c = col[:, None] * inv_freq
    cos = jnp.concatenate([jnp.cos(ang_r), jnp.cos(ang_r), jnp.cos(ang_c), jnp.cos(ang_c)], axis=1)
    sin = jnp.concatenate([-jnp.sin(ang_r), jnp.sin(ang_r), -jnp.sin(ang_c), jnp.sin(ang_c)], axis=1)
    reps = LANES // HEAD_DIM
    return jnp.tile(cos, (1, reps)), jnp.tile(sin, (1, reps))


def _block_diag_ones(width):
    i = jnp.arange(width) // HEAD_DIM
    return (i[:, None] == i[None, :]).astype(BF16)


def _pick(n, prefs):
    for p in prefs:
        if n % p == 0:
            return p
    raise ValueError(f"no block size in {prefs} divides {n}")


def _layer(x, prm):
    b, n, _ = x.shape
    t = b * n
    cap = CAPACITY_FACTOR * t // N_EXPERTS
    assert n % (WKV_CHUNKS_PER_STEP * CHUNK) == 0 and n % GRID_W == 0 and t % LANES == 0 and cap % LANES == 0
    x2 = x.reshape(t, D_MODEL)
    cos_t, sin_t = _rope_tables(n)

    tm = _pick(n, (512, 256, 128, 64))
    feat, q, k, vt = _in_proj(x2, prm["norm1"], prm["w_in"], prm["qk_gain"], cos_t, sin_t, prm["seg_qk"], n, tm)

    tq = _pick(n, (512, 256, 128))
    tk = _pick(n, (1024, 512, 256, 128))
    attn_out = _attention(q, k, vt, b, n, tq, tk)

    y_f, y_r = _wkv(feat, prm["taps"], prm["w0"], prm["w_up"], prm["a0"], prm["a_up"], prm["k_k"], prm["k_a"],
                    prm["seg_r"], b, n)
    tp = _pick(n, (256, 128, 64))
    rwkv_out = _rwkv_post(feat, y_f, y_r, prm["taps"], prm["a0"], prm["a_up"], prm["k_a"], prm["r_k"],
                          prm["g_up"], prm["ln_g"], prm["ln_b"], prm["seg_r"], n, tp)

    h, xn, aff_t = _out_proj(x2, rwkv_out, attn_out, prm["w_out"], prm["norm2"], prm["wr_hi"], prm["wr_lo"], tm)

    inv, rb = _topk(aff_t, cap)
    rb_flat = rb.reshape(-1)
    idx = _compact(rb_flat, inv, cap)
    ts = _pick(cap, (256, 128))
    eout = _experts(idx.reshape(N_EXPERTS, cap), xn, prm["w_gate"], prm["w_up_e"], prm["w_down"], cap, ts)
    y = _combine(rb_flat, h, inv, aff_t, eout, cap)
    return y.reshape(b, n, D_MODEL)


def _prepare(norm1, w_in, shift_taps, w0, w_lora_up, a0, a_lora_up, k_k, k_a, r_k, g_lora_up, ln_g, ln_b,
             q_norm, k_norm, w_out, norm2, w_router, expert_w_gate, expert_w_up, expert_w_down):
    l = 0
    wr = jnp.pad(w_router[l], ((0, 0), (0, LANES - N_EXPERTS)))
    wr_hi = wr.astype(BF16)
    wr_lo = (wr - wr_hi.astype(F32)).astype(BF16)
    qk_gain = jnp.concatenate([jnp.tile(q_norm[l], ATTN_Q_HEADS), jnp.tile(k_norm[l], ATTN_KV_HEADS)])
    return dict(
        norm1=norm1[l].reshape(1, D_MODEL),
        w_in=w_in[l].astype(BF16),
        taps=shift_taps[l],
        w0=w0[l],
        w_up=w_lora_up[l].astype(BF16),
        a0=a0[l],
        a_up=a_lora_up[l].astype(BF16),
        k_k=k_k[l].reshape(1, RWKV_WIDTH),
        k_a=k_a[l].reshape(1, RWKV_WIDTH),
        r_k=r_k[l].reshape(1, RWKV_WIDTH),
        g_up=g_lora_up[l].astype(BF16),
        ln_g=ln_g[l].reshape(1, RWKV_WIDTH),
        ln_b=ln_b[l].reshape(1, RWKV_WIDTH),
        qk_gain=qk_gain.reshape(1, QK_COLS),
        w_out=w_out[l].astype(BF16),
        norm2=norm2[l].reshape(1, D_MODEL),
        wr_hi=wr_hi,
        wr_lo=wr_lo,
        w_gate=expert_w_gate[l].astype(BF16),
        w_up_e=expert_w_up[l].astype(BF16),
        w_down=expert_w_down[l].astype(BF16),
        seg_qk=_block_diag_ones(QK_COLS),
        seg_r=_block_diag_ones(RWKV_WIDTH),
    )


def kernel(x_prompt, x_sample, norm1, w_in, shift_taps, w0, w_lora_up, a0, a_lora_up, k_k, k_a, r_k, g_lora_up,
           ln_g, ln_b, q_norm, k_norm, w_out, norm2, w_router, expert_w_gate, expert_w_up, expert_w_down):
    assert norm1.shape[0] == 1, "single-layer trunk"
    prm = _prepare(norm1, w_in, shift_taps, w0, w_lora_up, a0, a_lora_up, k_k, k_a, r_k, g_lora_up, ln_g, ln_b,
                   q_norm, k_norm, w_out, norm2, w_router, expert_w_gate, expert_w_up, expert_w_down)
    return (_layer(x_prompt, prm), _layer(x_sample, prm))
```

```python
import functools
import math

import jax
import jax.numpy as jnp
from jax import lax
from jax.experimental import pallas as pl
from jax.experimental.pallas import tpu as pltpu

F32 = jnp.float32
BF16 = jnp.bfloat16
I32 = jnp.int32

D_MODEL = 1024
HEAD_DIM = 64
RWKV_WIDTH = 512
RWKV_HEADS = RWKV_WIDTH // HEAD_DIM
ATTN_WIDTH = D_MODEL - RWKV_WIDTH
ATTN_Q_HEADS = ATTN_WIDTH // HEAD_DIM
ATTN_KV_HEADS = 2
KV_GROUP = ATTN_Q_HEADS // ATTN_KV_HEADS
KV_WIDTH = ATTN_KV_HEADS * HEAD_DIM
W_LORA = 64
A_LORA = 64
G_LORA = 128
RWKV_COLS = 3 * RWKV_WIDTH + W_LORA + A_LORA + G_LORA
IN_COLS = RWKV_COLS + ATTN_WIDTH + 2 * KV_WIDTH
QK_COLS = ATTN_WIDTH + KV_WIDTH
GRID_W = 64
ROPE_THETA = 10000.0
ROPE_PAIRS = HEAD_DIM // 4
N_EXPERTS = 16
CAPACITY_FACTOR = 2
D_EXPERT = 1024
NORM_EPS = 1e-6
GN_EPS = 64e-5
KK_EPS_SQ = 1e-24
DECAY_SCALE = math.exp(-0.5)
Q_SCALE = math.log2(math.e) / math.sqrt(HEAD_DIM)

LANES = 128
CHUNK = 64
PAIR = 2 * HEAD_DIM
N_PAIRS = RWKV_WIDTH // PAIR
WKV_CHUNKS_PER_STEP = 4
QSUB = 256
KEY_CHUNK = 1024
VROWS = 80
ATTN_PIPE_WIDTH = 2
CHUNK_ALIGN = 16
CHUNK_ROWS = LANES + CHUNK_ALIGN
VMEM_LIMIT = 48 * 1024 * 1024


def _cparams(sem):
    return pltpu.CompilerParams(dimension_semantics=sem, vmem_limit_bytes=VMEM_LIMIT)


def _full(shape):
    nd = len(shape)
    return pl.BlockSpec(shape, lambda *_: (0,) * nd)


def _dot(a, b):
    return jnp.dot(a, b, preferred_element_type=F32)


def _dot_nt(a, b):
    return lax.dot_general(a, b, (((1,), (1,)), ((), ())), preferred_element_type=F32)


def _split2(x):
    hi = x.astype(BF16)
    lo = (x - hi.astype(F32)).astype(BF16)
    return hi, lo


def _split3(x):
    hi = x.astype(BF16)
    r1 = x - hi.astype(F32)
    mid = r1.astype(BF16)
    lo = (r1 - mid.astype(F32)).astype(BF16)
    return hi, mid, lo


def _seg_sum(x, seg):
    hi, lo = _split2(x)
    return _dot(hi, seg) + _dot(lo, seg)


def _sigmoid(x):
    return 1.0 / (1.0 + jnp.exp(-x))


def _in_proj_kernel(x_ref, g_ref, w_ref, qkg_ref, cos_ref, sin_ref, seg_ref,
                    feat_ref, q_ref, k_ref, vt_ref):
    x = x_ref[...]
    ms = jnp.mean(x * x, axis=-1, keepdims=True)
    xn = (x * lax.rsqrt(ms + NORM_EPS) * g_ref[...]).astype(BF16)
    p = _dot(xn, w_ref[...])
    feat_ref[...] = p[:, :RWKV_COLS]
    qk = p[:, RWKV_COLS:RWKV_COLS + QK_COLS]
    ssq = _seg_sum(qk * qk, seg_ref[...])
    qkn = qk * lax.rsqrt(ssq * (1.0 / HEAD_DIM) + NORM_EPS) * qkg_ref[...]
    reps = QK_COLS // LANES
    cos = jnp.concatenate([cos_ref[...]] * reps, axis=1)
    sin = jnp.concatenate([sin_ref[...]] * reps, axis=1)
    lane = lax.broadcasted_iota(I32, qkn.shape, 1)
    first = (lane & (2 * ROPE_PAIRS - 1)) < ROPE_PAIRS
    partner = jnp.where(first, pltpu.roll(qkn, QK_COLS - ROPE_PAIRS, 1), pltpu.roll(qkn, ROPE_PAIRS, 1))
    rot = qkn * cos + partner * sin
    q_ref[...] = (rot[:, :ATTN_WIDTH] * Q_SCALE).astype(BF16)
    kr = rot[:, ATTN_WIDTH:].astype(BF16)
    vt = p[:, RWKV_COLS + QK_COLS:].T
    row = lax.broadcasted_iota(I32, vt.shape, 0)
    for j in range(ATTN_KV_HEADS):
        k_ref[j] = kr[:, j * HEAD_DIM:(j + 1) * HEAD_DIM]
        vj = vt if j == 0 else pltpu.roll(vt, (ATTN_KV_HEADS - j) * HEAD_DIM, 0)
        vt_ref[j] = jnp.where(row < HEAD_DIM, vj, jnp.where(row == HEAD_DIM, 1.0, 0.0)).astype(BF16)


def _in_proj(x2, norm1, w_in, qk_gain, cos_t, sin_t, seg_qk, n, tm):
    t = x2.shape[0]
    nblk = n // tm
    return pl.pallas_call(
        _in_proj_kernel,
        grid=(t // tm,),
        in_specs=[
            pl.BlockSpec((tm, D_MODEL), lambda i: (i, 0)),
            _full((1, D_MODEL)),
            _full((D_MODEL, IN_COLS)),
            _full((1, QK_COLS)),
            pl.BlockSpec((tm, LANES), lambda i: (i % nblk, 0)),
            pl.BlockSpec((tm, LANES), lambda i: (i % nblk, 0)),
            _full((QK_COLS, QK_COLS)),
        ],
        out_specs=[
            pl.BlockSpec((tm, RWKV_COLS), lambda i: (i, 0)),
            pl.BlockSpec((tm, ATTN_WIDTH), lambda i: (i, 0)),
            pl.BlockSpec((ATTN_KV_HEADS, tm, HEAD_DIM), lambda i: (0, i, 0)),
            pl.BlockSpec((ATTN_KV_HEADS, LANES, tm), lambda i: (0, 0, i)),
        ],
        out_shape=[
            jax.ShapeDtypeStruct((t, RWKV_COLS), F32),
            jax.ShapeDtypeStruct((t, ATTN_WIDTH), BF16),
            jax.ShapeDtypeStruct((ATTN_KV_HEADS, t, HEAD_DIM), BF16),
            jax.ShapeDtypeStruct((ATTN_KV_HEADS, LANES, t), BF16),
        ],
        compiler_params=_cparams(("parallel",)),
        name="in_proj",
    )(x2, norm1, w_in, qk_gain, cos_t, sin_t, seg_qk)


def _attn_kernel(q_ref, k_ref, vt_ref, o_ref, qt_ref, m_ref, acc_ref, *, tq, tk, kb):
    ki = pl.program_id(2)
    nsub = KV_GROUP * tq // QSUB

    @pl.when(ki == 0)
    def _():
        qt = q_ref[...].astype(F32).T
        for j in range(ATTN_KV_HEADS):
            for g in range(KV_GROUP):
                h = j * KV_GROUP + g
                qt_ref[j, :, g * tq:(g + 1) * tq] = qt[h * HEAD_DIM:(h + 1) * HEAD_DIM, :].astype(BF16)
        m_ref[...] = jnp.full(m_ref.shape, -jnp.inf, F32)
        acc_ref[...] = jnp.zeros(acc_ref.shape, F32)

    tiles = [(j, s * QSUB, c * kb) for c in range(tk // kb) for j in range(ATTN_KV_HEADS) for s in range(nsub)]
    scores, probs, alphas = {}, {}, {}
    width = ATTN_PIPE_WIDTH
    ngroups = len(tiles) // width

    def group(gidx):
        return range(gidx * width, (gidx + 1) * width) if 0 <= gidx < ngroups else ()

    for step in range(ngroups + 2):
        for i in group(step):
            j, q0, k0 = tiles[i]
            scores[i] = _dot(k_ref[j, k0:k0 + kb, :], qt_ref[j, :, q0:q0 + QSUB])
        for i in group(step - 1):
            j, q0, k0 = tiles[i]
            st = scores.pop(i)
            m = m_ref[j, :, q0:q0 + QSUB]
            m_new = jnp.maximum(m, jnp.max(st, axis=0, keepdims=True))
            probs[i] = jnp.exp2(st - m_new).astype(BF16)
            alphas[i] = jnp.exp2(m - m_new)
            m_ref[j, :, q0:q0 + QSUB] = m_new
        for i in group(step - 2):
            j, q0, k0 = tiles[i]
            acc_ref[j, :, q0:q0 + QSUB] = (alphas.pop(i) * acc_ref[j, :, q0:q0 + QSUB]
                                           + _dot(vt_ref[j, 0:VROWS, k0:k0 + kb], probs.pop(i)))

    @pl.when(ki == pl.num_programs(2) - 1)
    def _():
        outs = []
        for j in range(ATTN_KV_HEADS):
            for g in range(KV_GROUP):
                a = acc_ref[j, :, g * tq:(g + 1) * tq]
                outs.append(a[:HEAD_DIM, :] / a[HEAD_DIM:HEAD_DIM + 1, :])
        o_ref[...] = jnp.concatenate(outs, axis=0).T.astype(o_ref.dtype)


def _attention(q, k, vt, b, n, tq, tk):
    t = b * n
    nq, nk = n // tq, n // tk
    nqry = KV_GROUP * tq
    kb = min(tk, KEY_CHUNK)
    return pl.pallas_call(
        functools.partial(_attn_kernel, tq=tq, tk=tk, kb=kb),
        grid=(b, nq, nk),
        in_specs=[
            pl.BlockSpec((tq, ATTN_WIDTH), lambda bi, qi, ki: (bi * nq + qi, 0)),
            pl.BlockSpec((ATTN_KV_HEADS, tk, HEAD_DIM), lambda bi, qi, ki: (0, bi * nk + ki, 0)),
            pl.BlockSpec((ATTN_KV_HEADS, LANES, tk), lambda bi, qi, ki: (0, 0, bi * nk + ki)),
        ],
        out_specs=pl.BlockSpec((tq, ATTN_WIDTH), lambda bi, qi, ki: (bi * nq + qi, 0)),
        out_shape=jax.ShapeDtypeStruct((t, ATTN_WIDTH), BF16),
        scratch_shapes=[
            pltpu.VMEM((ATTN_KV_HEADS, HEAD_DIM, nqry), BF16),
            pltpu.VMEM((ATTN_KV_HEADS, 1, nqry), F32),
            pltpu.VMEM((ATTN_KV_HEADS, VROWS, nqry), F32),
        ],
        compiler_params=_cparams(("parallel", "parallel", "arbitrary")),
        name="attention",
    )(q, k, vt)


def _shift_feat(p, prev_row, next_row, taps):
    c = p.shape[0]
    rows = lax.broadcasted_iota(I32, p.shape, 0)
    prev = jnp.where(rows == 0, prev_row, pltpu.roll(p, 1, 0))
    nxt = jnp.where(rows == c - 1, next_row, pltpu.roll(p, c - 1, 0))
    return taps[0:1] * prev + taps[1:2] * p + taps[2:3] * nxt


def _split_feat(feat):
    r = feat[:, :RWKV_WIDTH]
    k = feat[:, RWKV_WIDTH:2 * RWKV_WIDTH]
    v = feat[:, 2 * RWKV_WIDTH:3 * RWKV_WIDTH]
    o = 3 * RWKV_WIDTH
    wd = feat[:, o:o + W_LORA]
    ad = feat[:, o + W_LORA:o + W_LORA + A_LORA]
    gd = feat[:, o + W_LORA + A_LORA:]
    return r, k, v, wd, ad, gd


def _in_context_rate(ad_b, a0_row, a_up):
    return _sigmoid(a0_row + _dot(ad_b, a_up))


def _halo_rows(hp_ref, hn_ref, is_first, is_last):
    prev_row = jnp.where(is_first, 0.0, hp_ref[7:8, :])
    next_row = jnp.where(is_last, 0.0, hn_ref[0:1, :])
    return prev_row, next_row


def _masked_stack(x):
    lane = lax.broadcasted_iota(I32, x.shape, 1)
    return jnp.concatenate([jnp.where(lane < HEAD_DIM, x, 0.0), jnp.where(lane >= HEAD_DIM, x, 0.0)], axis=0)


def _wkv_chunks(ops, s_ref):
    c = CHUNK
    ti = lax.broadcasted_iota(I32, (c, c), 0)
    si = lax.broadcasted_iota(I32, (c, c), 1)
    rr = lax.broadcasted_iota(I32, (PAIR, PAIR), 0)
    cc = lax.broadcasted_iota(I32, (PAIR, PAIR), 1)
    eye = (rr == cc).astype(F32)
    rr = rr & (c - 1)
    cc = cc & (c - 1)

    chains = []
    for d, seq, (r, kd, v, an, bb, lw) in ops:
        rev = d == 1
        l_incl = ((si >= ti) if rev else (si <= ti)).astype(BF16)
        lw_hi, lw_mid, lw_lo = _split3(lw)
        cum = _dot(l_incl, lw_hi) + _dot(l_incl, lw_mid) + _dot(l_incl, lw_lo)
        tot = jnp.sum(lw, axis=0, keepdims=True)
        e_neg = jnp.exp(-cum)
        e_tot = jnp.exp(tot - cum)
        sc = dict(a=an * jnp.exp(cum - lw), r=r * jnp.exp(cum), b=bb * e_neg, k=kd * e_neg,
                  bh=bb * e_tot, kh=kd * e_tot, v=v)
        g = jnp.exp(tot)
        tri_incl = (cc >= rr) if rev else (cc <= rr)
        tri_strict = (cc > rr) if rev else (cc < rr)
        for p in range(N_PAIRS):
            sl = slice(p * PAIR, (p + 1) * PAIR)
            ch = dict(d=d, seq=seq, p=p, tri_incl=tri_incl, tri_strict=tri_strict, g=g[:, sl])
            for name in ("a", "b", "k", "bh", "kh", "v"):
                ch[name] = _masked_stack(sc[name][:, sl]).astype(BF16)
            ch["r"] = _masked_stack(sc["r"][:, sl])
            chains.append(ch)

    for ch in chains:
        gram = _dot_nt(jnp.concatenate([ch["a"], ch["r"].astype(BF16)], axis=0),
                       jnp.concatenate([ch["b"], ch["k"]], axis=0))
        ch["x"] = jnp.where(ch["tri_strict"], gram[:PAIR, :PAIR], 0.0)
        ch["m_ak"] = jnp.where(ch["tri_strict"], gram[:PAIR, PAIR:], 0.0).astype(BF16)
        ch["m_rb"] = jnp.where(ch["tri_incl"], gram[PAIR:, :PAIR], 0.0).astype(BF16)
        ch["m_rk"] = jnp.where(ch["tri_incl"], gram[PAIR:, PAIR:], 0.0).astype(BF16)
        ch["t_inv"] = eye

    nsteps = int(math.log2(c))
    for step in range(nsteps):
        last = step == nsteps - 1
        for ch in chains:
            yb = ch["x"].astype(BF16)
            tb = ch["t_inv"].astype(BF16)
            z = _dot(yb, tb if last else jnp.concatenate([yb, tb], axis=1))
            if last:
                ch["t_inv"] = ch["t_inv"] + z
            else:
                ch["x"] = z[:, :PAIR]
                ch["t_inv"] = ch["t_inv"] + z[:, PAIR:]

    for ch in chains:
        ch["mv"] = _dot(ch["m_ak"], ch["v"]).astype(BF16)
        ch["vk"] = _dot(ch["v"].astype(F32).T.astype(BF16), ch["kh"])
    for ch in chains:
        ch["ta"] = _dot(ch["t_inv"].astype(BF16), jnp.concatenate([ch["a"], ch["mv"]], axis=1))
    for ch in chains:
        ch["gh"] = _dot(ch["ta"].T.astype(BF16), ch["bh"])
        ch["mr"] = _dot(ch["m_rb"], ch["ta"].astype(BF16))
        ch["y_in"] = _dot(ch["m_rk"], ch["v"])
    state = {}
    parts = {}
    for ch in sorted(chains, key=lambda ch: ch["seq"]):
        key = (ch["d"], ch["p"])
        s_in = state[key] if key in state else s_ref[ch["d"], ch["p"]]
        s_b = s_in.astype(BF16)
        q_h = (ch["r"] + ch["mr"][:, :PAIR]).astype(BF16)
        y_ms = _dot_nt(q_h, s_b) + ch["mr"][:, PAIR:] + ch["y_in"]
        parts.setdefault((ch["d"], ch["seq"]), []).append(y_ms[:c] + y_ms[c:])
        gh = ch["gh"]
        state[key] = s_in * ch["g"] + _dot(s_b, gh[:PAIR].astype(BF16)) + gh[PAIR:] + ch["vk"]
    for (d, p), s_out in state.items():
        s_ref[d, p] = s_out
    return {key: jnp.concatenate(y, axis=1) for key, y in parts.items()}


def _wkv_kernel(pf_ref, hpf_ref, hnf_ref, pr_ref, hpr_ref, hnr_ref,
                taps_ref, w0_ref, wup_ref, a0_ref, aup_ref, kk_ref, ka_ref, seg_ref,
                yf_ref, yr_ref, s_ref, *, cps):
    c = pl.program_id(1)
    nc = pl.num_programs(1)

    @pl.when(c == 0)
    def _():
        s_ref[...] = jnp.zeros(s_ref.shape, F32)

    taps = taps_ref[...]
    seg = seg_ref[...]
    refs = ((pf_ref, hpf_ref, hnf_ref), (pr_ref, hpr_ref, hnr_ref))
    blocks = (c, nc - 1 - c)
    dirs = (0, 1)
    feats = []
    for d in dirs:
        p_ref, hp_ref, hn_ref = refs[d]
        prev_row, next_row = _halo_rows(hp_ref, hn_ref, blocks[d] == 0, blocks[d] == nc - 1)
        feats.append(_split_feat(_shift_feat(p_ref[...], prev_row, next_row, taps)))
    kks = [f[1] * kk_ref[...] for f in feats]
    norms = [_seg_sum(kk * kk, seg) for kk in kks]
    wlin = [_dot(jnp.tanh(feats[d][3]).astype(BF16), wup_ref[d]) for d in dirs]
    alin = [_dot(feats[d][4].astype(BF16), aup_ref[d]) for d in dirs]
    ops = []
    for d in dirs:
        r, k, v = feats[d][:3]
        kk = kks[d] * lax.rsqrt(jnp.maximum(norms[d], KK_EPS_SQ))
        lw = -DECAY_SCALE * _sigmoid(w0_ref[d:d + 1, :] + wlin[d])
        a = _sigmoid(a0_ref[d:d + 1, :] + alin[d])
        kd = k * (1.0 + (a - 1.0) * ka_ref[...])
        full = (r, kd, v, -kk, kk * a, lw)
        for seq in range(cps):
            ci = seq if d == 0 else cps - 1 - seq
            ops.append((d, seq, tuple(x[ci * CHUNK:(ci + 1) * CHUNK] for x in full)))
    ys = _wkv_chunks(ops, s_ref)
    for d, y_ref in enumerate((yf_ref, yr_ref)):
        for seq in range(cps):
            ci = seq if d == 0 else cps - 1 - seq
            y_ref[ci * CHUNK:(ci + 1) * CHUNK, :] = ys[(d, seq)]


def _wkv(feat, taps, w0, w_up, a0, a_up, k_k, k_a, seg_r, b, n):
    t = b * n
    cps = WKV_CHUNKS_PER_STEP
    rows = cps * CHUNK
    nc = n // rows
    c8 = rows // 8
    last8 = t // 8 - 1

    def fwd(bi, ci):
        return bi * nc + ci

    def rev(bi, ci):
        return bi * nc + (nc - 1 - ci)

    def specs(block_of):
        return [
            pl.BlockSpec((rows, RWKV_COLS), lambda bi, ci: (block_of(bi, ci), 0)),
            pl.BlockSpec((8, RWKV_COLS), lambda bi, ci: (jnp.maximum(block_of(bi, ci) * c8 - 1, 0), 0)),
            pl.BlockSpec((8, RWKV_COLS), lambda bi, ci: (jnp.minimum((block_of(bi, ci) + 1) * c8, last8), 0)),
        ]

    return pl.pallas_call(
        functools.partial(_wkv_kernel, cps=cps),
        grid=(b, nc),
        in_specs=specs(fwd) + specs(rev) + [
            _full((3, RWKV_COLS)),
            _full((2, RWKV_WIDTH)),
            _full((2, W_LORA, RWKV_WIDTH)),
            _full((2, RWKV_WIDTH)),
            _full((2, A_LORA, RWKV_WIDTH)),
            _full((1, RWKV_WIDTH)),
            _full((1, RWKV_WIDTH)),
            _full((RWKV_WIDTH, RWKV_WIDTH)),
        ],
        out_specs=[
            pl.BlockSpec((rows, RWKV_WIDTH), lambda bi, ci: (fwd(bi, ci), 0)),
            pl.BlockSpec((rows, RWKV_WIDTH), lambda bi, ci: (rev(bi, ci), 0)),
        ],
        out_shape=[jax.ShapeDtypeStruct((t, RWKV_WIDTH), F32)] * 2,
        scratch_shapes=[pltpu.VMEM((2, N_PAIRS, PAIR, PAIR), F32)],
        compiler_params=_cparams(("parallel", "arbitrary")),
        name="wkv",
    )(feat, feat, feat, feat, feat, feat, taps, w0, w_up, a0, a_up, k_k, k_a, seg_r)


def _rwkv_post_kernel(p_ref, hp_ref, hn_ref, yf_ref, yr_ref, taps_ref, a0_ref, aup_ref, ka_ref, rk_ref,
                      gup_ref, lng_ref, lnb_ref, seg_ref, o_ref, *, blocks_per_seq):
    i = pl.program_id(0)
    pos = i % blocks_per_seq
    prev_row, next_row = _halo_rows(hp_ref, hn_ref, pos == 0, pos == blocks_per_seq - 1)
    feat = _shift_feat(p_ref[...], prev_row, next_row, taps_ref[...])
    r, k, v, _, ad, gd = _split_feat(feat)
    seg = seg_ref[...]
    ad_b = ad.astype(BF16)
    a_sum = (_in_context_rate(ad_b, a0_ref[0:1, :], aup_ref[0])
             + _in_context_rate(ad_b, a0_ref[1:2, :], aup_ref[1]))
    k_bonus = k * (1.0 + (0.5 * a_sum - 1.0) * ka_ref[...])
    gate = _dot(_sigmoid(gd).astype(BF16), gup_ref[...])
    y = yf_ref[...] + yr_ref[...]
    mu = _seg_sum(y, seg) * (1.0 / HEAD_DIM)
    yc = y - mu
    var = _seg_sum(yc * yc, seg) * (1.0 / HEAD_DIM)
    yn = yc * lax.rsqrt(var + GN_EPS) * lng_ref[...] + lnb_ref[...]
    bonus = _seg_sum(r * k_bonus * rk_ref[...], seg) * v
    o_ref[...] = ((yn + bonus) * gate).astype(o_ref.dtype)


def _rwkv_post(feat, y_f, y_r, taps, a0, a_up, k_a, r_k, g_up, ln_g, ln_b, seg_r, n, tm):
    t = feat.shape[0]
    bps = n // tm
    t8 = tm // 8
    last8 = t // 8 - 1
    row = lambda i: (i, 0)
    return pl.pallas_call(
        functools.partial(_rwkv_post_kernel, blocks_per_seq=bps),
        grid=(t // tm,),
        in_specs=[
            pl.BlockSpec((tm, RWKV_COLS), row),
            pl.BlockSpec((8, RWKV_COLS), lambda i: (jnp.maximum(i * t8 - 1, 0), 0)),
            pl.BlockSpec((8, RWKV_COLS), lambda i: (jnp.minimum((i + 1) * t8, last8), 0)),
            pl.BlockSpec((tm, RWKV_WIDTH), row),
            pl.BlockSpec((tm, RWKV_WIDTH), row),
            _full((3, RWKV_COLS)),
            _full((2, RWKV_WIDTH)),
            _full((2, A_LORA, RWKV_WIDTH)),
            _full((1, RWKV_WIDTH)),
            _full((1, RWKV_WIDTH)),
            _full((G_LORA, RWKV_WIDTH)),
            _full((1, RWKV_WIDTH)),
            _full((1, RWKV_WIDTH)),
            _full((RWKV_WIDTH, RWKV_WIDTH)),
        ],
        out_specs=pl.BlockSpec((tm, RWKV_WIDTH), row),
        out_shape=jax.ShapeDtypeStruct((t, RWKV_WIDTH), BF16),
        compiler_params=_cparams(("parallel",)),
        name="rwkv_post",
    )(feat, feat, feat, y_f, y_r, taps, a0, a_up, k_a, r_k, g_up, ln_g, ln_b, seg_r)


def _out_proj_kernel(x_ref, ro_ref, ao_ref, wo_ref, g2_ref, wrh_ref, wrl_ref, h_ref, xn_ref, aff_ref):
    mix = _dot(ro_ref[...], wo_ref[:RWKV_WIDTH, :]) + _dot(ao_ref[...], wo_ref[RWKV_WIDTH:, :])
    h = x_ref[...] + mix
    h_ref[...] = h
    ms = jnp.mean(h * h, axis=-1, keepdims=True)
    xn = h * lax.rsqrt(ms + NORM_EPS) * g2_ref[...]
    xn_ref[...] = xn
    xh, xl = _split2(xn)
    wrh = wrh_ref[...]
    logits = _dot(xh, wrh) + _dot(xl, wrh) + _dot(xh, wrl_ref[...])
    lane = lax.broadcasted_iota(I32, logits.shape, 1)
    logits = jnp.where(lane < N_EXPERTS, logits, -jnp.inf)
    e = jnp.exp(logits - jnp.max(logits, axis=-1, keepdims=True))
    aff = e / jnp.sum(e, axis=-1, keepdims=True)
    aff_ref[...] = aff.T[:N_EXPERTS, :]


def _out_proj(x2, rwkv_out, attn_out, w_out, norm2, wr_hi, wr_lo, tm):
    t = x2.shape[0]
    row = lambda i: (i, 0)
    return pl.pallas_call(
        _out_proj_kernel,
        grid=(t // tm,),
        in_specs=[
            pl.BlockSpec((tm, D_MODEL), row),
            pl.BlockSpec((tm, RWKV_WIDTH), row),
            pl.BlockSpec((tm, ATTN_WIDTH), row),
            _full((D_MODEL, D_MODEL)),
            _full((1, D_MODEL)),
            _full((D_MODEL, LANES)),
            _full((D_MODEL, LANES)),
        ],
        out_specs=[
            pl.BlockSpec((tm, D_MODEL), row),
            pl.BlockSpec((tm, D_MODEL), row),
            pl.BlockSpec((N_EXPERTS, tm), lambda i: (0, i)),
        ],
        out_shape=[
            jax.ShapeDtypeStruct((t, D_MODEL), F32),
            jax.ShapeDtypeStruct((t, D_MODEL), F32),
            jax.ShapeDtypeStruct((N_EXPERTS, t), F32),
        ],
        compiler_params=_cparams(("parallel",)),
        name="out_proj",
    )(x2, rwkv_out, attn_out, w_out, norm2, wr_hi, wr_lo)


def _topk_kernel(aff_ref, inv_ref, rb_ref, *, cap, t, tb):
    nrows = t // LANES
    nblk = t // tb
    rows_per_blk = tb // LANES

    def bits(sl):
        return pltpu.bitcast(aff_ref[:, sl], I32)

    def bisect(i, prefix):
        cand = prefix | jnp.left_shift(jnp.int32(1), 30 - i)
        cnt = jnp.sum((bits(slice(None)) >= cand).astype(I32), axis=1, keepdims=True)
        return jnp.where(cnt >= cap, cand, prefix)

    tau = lax.fori_loop(0, 31, bisect, jnp.zeros((N_EXPERTS, 1), I32))
    n_gt = jnp.sum((bits(slice(None)) > tau).astype(I32), axis=1, keepdims=True)
    need = (cap - n_gt).astype(F32)

    upper = (lax.broadcasted_iota(I32, (tb, tb), 0) < lax.broadcasted_iota(I32, (tb, tb), 1)).astype(BF16)
    tok_row = lax.broadcasted_iota(I32, (tb, nrows), 0) >> 7
    out_row = lax.broadcasted_iota(I32, (tb, nrows), 1)

    def block(b, carry):
        c_eq, c_pos, row_cnt = carry
        start = pl.multiple_of(b * tb, tb)
        xs = bits(pl.ds(start, tb))
        gt = xs > tau
        eq = xs == tau
        eq_b = eq.astype(BF16)
        rank_eq = c_eq + _dot(eq_b, upper)
        sel = gt | (eq & (rank_eq < need))
        sel_b = sel.astype(BF16)
        pos = c_pos + _dot(sel_b, upper)
        inv_ref[:, pl.ds(start, tb)] = jnp.where(sel, pos.astype(I32), -1)
        place = (out_row == tok_row + b * rows_per_blk).astype(BF16)
        row_cnt = row_cnt + _dot(sel_b, place)
        c_eq = c_eq + jnp.sum(eq_b.astype(F32), axis=1, keepdims=True)
        c_pos = c_pos + jnp.sum(sel_b.astype(F32), axis=1, keepdims=True)
        return c_eq, c_pos, row_cnt

    zero = jnp.zeros((N_EXPERTS, 1), F32)
    _, _, row_cnt = lax.fori_loop(0, nblk, block, (zero, zero, jnp.zeros((N_EXPERTS, nrows), F32)))
    upper_r = (lax.broadcasted_iota(I32, (nrows, nrows), 0) < lax.broadcasted_iota(I32, (nrows, nrows), 1)).astype(BF16)
    rb_ref[...] = _dot(row_cnt.astype(BF16), upper_r).astype(I32)


def _topk(aff_t, cap):
    t = aff_t.shape[1]
    tb = min(512, t)
    return pl.pallas_call(
        functools.partial(_topk_kernel, cap=cap, t=t, tb=tb),
        in_specs=[_full((N_EXPERTS, t))],
        out_specs=[_full((N_EXPERTS, t)), _full((N_EXPERTS, t // LANES))],
        out_shape=[jax.ShapeDtypeStruct((N_EXPERTS, t), I32), jax.ShapeDtypeStruct((N_EXPERTS, t // LANES), I32)],
        compiler_params=pltpu.CompilerParams(vmem_limit_bytes=VMEM_LIMIT),
        name="topk",
    )(aff_t)


def _compact_kernel(rb_ref, inv_ref, idx_ref, acc_ref, *, nrows, ntiles):
    e = pl.program_id(0)
    acc_ref[...] = jnp.zeros(acc_ref.shape, F32)
    sub = lax.broadcasted_iota(I32, (8, LANES), 0)
    lane_tok = lax.broadcasted_iota(I32, (8, LANES), 1)
    slot_off = lax.broadcasted_iota(I32, (LANES, LANES), 0)

    def body(r, _):
        start = pl.multiple_of(r * LANES, LANES)
        inv_row = inv_ref[0, :, pl.ds(start, LANES)]
        tok = lane_tok + r * LANES
        lmat = jnp.where(sub == 0, (tok >> 8).astype(F32),
                         jnp.where(sub == 1, (tok & 255).astype(F32), 0.0)).astype(BF16)
        q0 = rb_ref[e * nrows + r] >> 7
        inv_b = jnp.broadcast_to(inv_row, (LANES, LANES))
        for dq in range(2):
            onehot_t = (inv_b == slot_off + (q0 + dq) * LANES).astype(BF16)
            acc_ref[q0 + dq] += _dot_nt(lmat, onehot_t)
        return 0

    lax.fori_loop(0, nrows, body, 0, unroll=8)
    idx_ref[0] = (acc_ref[0:ntiles, 0, :] * 256.0 + acc_ref[0:ntiles, 1, :]).astype(I32)


def _compact(rb_flat, inv, cap):
    t = inv.shape[1]
    nrows = t // LANES
    ntiles = cap // LANES
    return pl.pallas_call(
        functools.partial(_compact_kernel, nrows=nrows, ntiles=ntiles),
        grid_spec=pltpu.PrefetchScalarGridSpec(
            num_scalar_prefetch=1,
            grid=(N_EXPERTS,),
            in_specs=[pl.BlockSpec((1, 1, t), lambda e, rb: (e, 0, 0))],
            out_specs=pl.BlockSpec((1, ntiles, LANES), lambda e, rb: (e, 0, 0)),
            scratch_shapes=[pltpu.VMEM((ntiles + 2, 8, LANES), F32)],
        ),
        out_shape=jax.ShapeDtypeStruct((N_EXPERTS, ntiles, LANES), I32),
        compiler_params=_cparams(("arbitrary",)),
        name="compact",
    )(rb_flat, inv.reshape(N_EXPERTS, 1, t))


def _expert_kernel(idx_ref, idxn_ref, x_hbm, wg_ref, wu_ref, wd_ref, o_ref, xbuf, sems, *, ts, nsteps):
    g = pl.program_id(0) * pl.num_programs(1) + pl.program_id(1)
    slot = g % 2

    def row_copy(ids, j, sl):
        return pltpu.make_async_copy(x_hbm.at[pl.ds(ids[0, 0, j], 1)], xbuf.at[sl, pl.ds(j, 1)], sems.at[sl])

    @pl.when(g == 0)
    def _():
        for j in range(ts):
            row_copy(idx_ref, j, 0).start()

    for j in range(ts):
        row_copy(idx_ref, j, slot).wait()
    quarter = ts // 4
    for j in range(0, quarter):
        row_copy(idxn_ref, j, 1 - slot).start()
    xs = xbuf[slot].astype(BF16)
    gp = _dot(xs, wg_ref[0])
    for j in range(quarter, 2 * quarter):
        row_copy(idxn_ref, j, 1 - slot).start()
    up = _dot(xs, wu_ref[0])
    for j in range(2 * quarter, 3 * quarter):
        row_copy(idxn_ref, j, 1 - slot).start()
    hid = (gp * _sigmoid(gp) * up).astype(BF16)
    o_ref[...] = _dot(hid, wd_ref[0]).astype(o_ref.dtype)
    for j in range(3 * quarter, ts):
        row_copy(idxn_ref, j, 1 - slot).start()

    @pl.when(g == nsteps - 1)
    def _():
        for j in range(ts):
            row_copy(idxn_ref, j, 1 - slot).wait()


def _experts(idx, xn, w_gate, w_up, w_down, cap, ts):
    nt = cap // ts
    nsteps = N_EXPERTS * nt
    idx3 = idx.reshape(nsteps, 1, ts)
    wspec = lambda a, b: pl.BlockSpec((1, a, b), lambda e, i: (e, 0, 0))
    return pl.pallas_call(
        functools.partial(_expert_kernel, ts=ts, nsteps=nsteps),
        grid=(N_EXPERTS, nt),
        in_specs=[
            pl.BlockSpec((1, 1, ts), lambda e, i: (e * nt + i, 0, 0), memory_space=pltpu.SMEM),
            pl.BlockSpec((1, 1, ts), lambda e, i: (jnp.minimum(e * nt + i + 1, nsteps - 1), 0, 0),
                         memory_space=pltpu.SMEM),
            pl.BlockSpec(memory_space=pl.ANY),
            wspec(D_MODEL, D_EXPERT),
            wspec(D_MODEL, D_EXPERT),
            wspec(D_EXPERT, D_MODEL),
        ],
        out_specs=pl.BlockSpec((ts, D_MODEL), lambda e, i: (e * nt + i, 0)),
        out_shape=jax.ShapeDtypeStruct((N_EXPERTS * cap, D_MODEL), BF16),
        scratch_shapes=[
            pltpu.VMEM((2, ts, D_MODEL), F32),
            pltpu.SemaphoreType.DMA((2,)),
        ],
        compiler_params=_cparams(("arbitrary", "arbitrary")),
        name="experts",
    )(idx3, idx3, xn, w_gate, w_up, w_down)


def _combine_kernel(rb_ref, h_ref, inv_ref, aff_ref, o_hbm, y_ref, cbuf, sems, *, nrows, cap):
    r = pl.program_id(0)
    slot = r % 2

    def chunk_copy(rr, e, sl):
        p0 = jnp.minimum(rb_ref[e * nrows + rr] & ~(CHUNK_ALIGN - 1), cap - CHUNK_ROWS)
        return pltpu.make_async_copy(o_hbm.at[e, pl.ds(pl.multiple_of(p0, CHUNK_ALIGN), CHUNK_ROWS), :],
                                     cbuf.at[sl, e], sems.at[sl]), p0

    @pl.when(r == 0)
    def _():
        for e in range(N_EXPERTS):
            chunk_copy(0, e, 0)[0].start()

    nxt = jnp.minimum(r + 1, nrows - 1)
    for e in range(N_EXPERTS):
        chunk_copy(nxt, e, 1 - slot)[0].start()

    pad = jnp.zeros((LANES - N_EXPERTS, LANES), F32)
    inv_t = jnp.concatenate([inv_ref[...].astype(F32), pad], axis=0).T
    aff_t = jnp.concatenate([aff_ref[...], pad], axis=0).T
    lane = lax.broadcasted_iota(I32, (LANES, CHUNK_ROWS), 1)
    acc = h_ref[...]
    bases = []
    for e in range(N_EXPERTS):
        cp, p0 = chunk_copy(r, e, slot)
        cp.wait()
        bases.append(p0)
    for e in range(N_EXPERTS):
        onehot = inv_t[:, e:e + 1] == (lane + bases[e]).astype(F32)
        gsel = jnp.where(onehot, aff_t[:, e:e + 1], 0.0).astype(BF16)
        acc = acc + _dot(gsel, cbuf[slot, e])
    y_ref[...] = acc

    @pl.when(r == nrows - 1)
    def _():
        for e in range(N_EXPERTS):
            chunk_copy(nxt, e, 1 - slot)[0].wait()


def _combine(rb_flat, h, inv, aff_t, eout, cap):
    t = h.shape[0]
    nrows = t // LANES
    return pl.pallas_call(
        functools.partial(_combine_kernel, nrows=nrows, cap=cap),
        grid_spec=pltpu.PrefetchScalarGridSpec(
            num_scalar_prefetch=1,
            grid=(nrows,),
            in_specs=[
                pl.BlockSpec((LANES, D_MODEL), lambda r, rb: (r, 0)),
                pl.BlockSpec((N_EXPERTS, LANES), lambda r, rb: (0, r)),
                pl.BlockSpec((N_EXPERTS, LANES), lambda r, rb: (0, r)),
                pl.BlockSpec(memory_space=pl.ANY),
            ],
            out_specs=pl.BlockSpec((LANES, D_MODEL), lambda r, rb: (r, 0)),
            scratch_shapes=[
                pltpu.VMEM((2, N_EXPERTS, CHUNK_ROWS, D_MODEL), BF16),
                pltpu.SemaphoreType.DMA((2,)),
            ],
        ),
        out_shape=jax.ShapeDtypeStruct((t, D_MODEL), F32),
        compiler_params=_cparams(("arbitrary",)),
        name="combine",
    )(rb_flat, h, inv, aff_t, eout.reshape(N_EXPERTS, cap, D_MODEL))


def _rope_tables(n):
    pos = jnp.arange(n, dtype=F32)
    row = jnp.floor(pos / GRID_W)
    col = pos - row * GRID_W
    inv_freq = ROPE_THETA ** (-jnp.arange(ROPE_PAIRS, dtype=F32) / ROPE_PAIRS)
    ang_r = row[:, None] * inv_freq
    ang_c = col[:, None] * inv_freq
    cos = jnp.concatenate([jnp.cos(ang_r), jnp.cos(ang_r), jnp.cos(ang_c), jnp.cos(ang_c)], axis=1)
    sin = jnp.concatenate([-jnp.sin(ang_r), jnp.sin(ang_r), -jnp.sin(ang_c), jnp.sin(ang_c)], axis=1)
    reps = LANES // HEAD_DIM
    return jnp.tile(cos, (1, reps)), jnp.tile(sin, (1, reps))


def _block_diag_ones(width):
    i = jnp.arange(width) // HEAD_DIM
    return (i[:, None] == i[None, :]).astype(BF16)


def _pick(n, prefs):
    for p in prefs:
        if n % p == 0:
            return p
    raise ValueError(f"no block size in {prefs} divides {n}")


def _layer(x, prm):
    b, n, _ = x.shape
    t = b * n
    cap = CAPACITY_FACTOR * t // N_EXPERTS
    assert n % (WKV_CHUNKS_PER_STEP * CHUNK) == 0 and n % GRID_W == 0 and t % LANES == 0 and cap % LANES == 0
    x2 = x.reshape(t, D_MODEL)
    cos_t, sin_t = _rope_tables(n)

    tm = _pick(n, (512, 256, 128, 64))
    feat, q, k, vt = _in_proj(x2, prm["norm1"], prm["w_in"], prm["qk_gain"], cos_t, sin_t, prm["seg_qk"], n, tm)

    tq = _pick(n, (1024, 512, 256, 128))
    tk = _pick(n, (2048, 1024, 512, 256, 128))
    attn_out = _attention(q, k, vt, b, n, tq, tk)

    y_f, y_r = _wkv(feat, prm["taps"], prm["w0"], prm["w_up"], prm["a0"], prm["a_up"], prm["k_k"], prm["k_a"],
                    prm["seg_r"], b, n)
    tp = _pick(n, (256, 128, 64))
    rwkv_out = _rwkv_post(feat, y_f, y_r, prm["taps"], prm["a0"], prm["a_up"], prm["k_a"], prm["r_k"],
                          prm["g_up"], prm["ln_g"], prm["ln_b"], prm["seg_r"], n, tp)

    h, xn, aff_t = _out_proj(x2, rwkv_out, attn_out, prm["w_out"], prm["norm2"], prm["wr_hi"], prm["wr_lo"], tm)

    inv, rb = _topk(aff_t, cap)
    rb_flat = rb.reshape(-1)
    idx = _compact(rb_flat, inv, cap)
    ts = _pick(cap, (256, 128))
    eout = _experts(idx.reshape(N_EXPERTS, cap), xn, prm["w_gate"], prm["w_up_e"], prm["w_down"], cap, ts)
    y = _combine(rb_flat, h, inv, aff_t, eout, cap)
    return y.reshape(b, n, D_MODEL)


def _prepare(norm1, w_in, shift_taps, w0, w_lora_up, a0, a_lora_up, k_k, k_a, r_k, g_lora_up, ln_g, ln_b,
             q_norm, k_norm, w_out, norm2, w_router, expert_w_gate, expert_w_up, expert_w_down):
    l = 0
    wr = jnp.pad(w_router[l], ((0, 0), (0, LANES - N_EXPERTS)))
    wr_hi = wr.astype(BF16)
    wr_lo = (wr - wr_hi.astype(F32)).astype(BF16)
    qk_gain = jnp.concatenate([jnp.tile(q_norm[l], ATTN_Q_HEADS), jnp.tile(k_norm[l], ATTN_KV_HEADS)])
    return dict(
        norm1=norm1[l].reshape(1, D_MODEL),
        w_in=w_in[l].astype(BF16),
        taps=shift_taps[l],
        w0=w0[l],
        w_up=w_lora_up[l].astype(BF16),
        a0=a0[l],
        a_up=a_lora_up[l].astype(BF16),
        k_k=k_k[l].reshape(1, RWKV_WIDTH),
        k_a=k_a[l].reshape(1, RWKV_WIDTH),
        r_k=r_k[l].reshape(1, RWKV_WIDTH),
        g_up=g_lora_up[l].astype(BF16),
        ln_g=ln_g[l].reshape(1, RWKV_WIDTH),
        ln_b=ln_b[l].reshape(1, RWKV_WIDTH),
        qk_gain=qk_gain.reshape(1, QK_COLS),
        w_out=w_out[l].astype(BF16),
        norm2=norm2[l].reshape(1, D_MODEL),
        wr_hi=wr_hi,
        wr_lo=wr_lo,
        w_gate=expert_w_gate[l].astype(BF16),
        w_up_e=expert_w_up[l].astype(BF16),
        w_down=expert_w_down[l].astype(BF16),
        seg_qk=_block_diag_ones(QK_COLS),
        seg_r=_block_diag_ones(RWKV_WIDTH),
    )


def kernel(x_prompt, x_sample, norm1, w_in, shift_taps, w0, w_lora_up, a0, a_lora_up, k_k, k_a, r_k, g_lora_up,
           ln_g, ln_b, q_norm, k_norm, w_out, norm2, w_router, expert_w_gate, expert_w_up, expert_w_down):
    assert norm1.shape[0] == 1, "single-layer trunk"
    prm = _prepare(norm1, w_in, shift_taps, w0, w_lora_up, a0, a_lora_up, k_k, k_a, r_k, g_lora_up, ln_g, ln_b,
                   q_norm, k_norm, w_out, norm2, w_router, expert_w_gate, expert_w_up, expert_w_down)
    return (_layer(x_prompt, prm), _layer(x_sample, prm))
```

```python
import functools
import math

import jax
import jax.numpy as jnp
from jax import lax
from jax.experimental import pallas as pl
from jax.experimental.pallas import tpu as pltpu

F32 = jnp.float32
BF16 = jnp.bfloat16
I32 = jnp.int32

D_MODEL = 1024
HEAD_DIM = 64
RWKV_WIDTH = 512
RWKV_HEADS = RWKV_WIDTH // HEAD_DIM
ATTN_WIDTH = D_MODEL - RWKV_WIDTH
ATTN_Q_HEADS = ATTN_WIDTH // HEAD_DIM
ATTN_KV_HEADS = 2
KV_GROUP = ATTN_Q_HEADS // ATTN_KV_HEADS
KV_WIDTH = ATTN_KV_HEADS * HEAD_DIM
W_LORA = 64
A_LORA = 64
G_LORA = 128
RWKV_COLS = 3 * RWKV_WIDTH + W_LORA + A_LORA + G_LORA
IN_COLS = RWKV_COLS + ATTN_WIDTH + 2 * KV_WIDTH
QK_COLS = ATTN_WIDTH + KV_WIDTH
GRID_W = 64
ROPE_THETA = 10000.0
ROPE_PAIRS = HEAD_DIM // 4
N_EXPERTS = 16
CAPACITY_FACTOR = 2
D_EXPERT = 1024
NORM_EPS = 1e-6
GN_EPS = 64e-5
KK_EPS_SQ = 1e-24
DECAY_SCALE = math.exp(-0.5)
Q_SCALE = math.log2(math.e) / math.sqrt(HEAD_DIM)

LANES = 128
CHUNK = 64
PAIR = 2 * HEAD_DIM
N_PAIRS = RWKV_WIDTH // PAIR
WKV_CHUNKS_PER_STEP = 4
QSUB = 256
KEY_CHUNK = 1024
VROWS = 80
ATTN_PIPE_WIDTH = 2
CHUNK_ALIGN = 16
CHUNK_ROWS = LANES + CHUNK_ALIGN
VMEM_LIMIT = 48 * 1024 * 1024


def _cparams(sem):
    return pltpu.CompilerParams(dimension_semantics=sem, vmem_limit_bytes=VMEM_LIMIT)


def _full(shape):
    nd = len(shape)
    return pl.BlockSpec(shape, lambda *_: (0,) * nd)


def _dot(a, b):
    return jnp.dot(a, b, preferred_element_type=F32)


def _dot_nt(a, b):
    return lax.dot_general(a, b, (((1,), (1,)), ((), ())), preferred_element_type=F32)


def _split2(x):
    hi = x.astype(BF16)
    lo = (x - hi.astype(F32)).astype(BF16)
    return hi, lo


def _split3(x):
    hi = x.astype(BF16)
    r1 = x - hi.astype(F32)
    mid = r1.astype(BF16)
    lo = (r1 - mid.astype(F32)).astype(BF16)
    return hi, mid, lo


def _seg_sum(x, seg):
    hi, lo = _split2(x)
    return _dot(hi, seg) + _dot(lo, seg)


def _sigmoid(x):
    return 1.0 / (1.0 + jnp.exp(-x))


def _in_proj_kernel(x_ref, g_ref, w_ref, qkg_ref, cos_ref, sin_ref, seg_ref,
                    feat_ref, q_ref, k_ref, vt_ref):
    x = x_ref[...]
    ms = jnp.mean(x * x, axis=-1, keepdims=True)
    xn = (x * lax.rsqrt(ms + NORM_EPS) * g_ref[...]).astype(BF16)
    p = _dot(xn, w_ref[...])
    feat_ref[...] = p[:, :RWKV_COLS]
    qk = p[:, RWKV_COLS:RWKV_COLS + QK_COLS]
    ssq = _seg_sum(qk * qk, seg_ref[...])
    qkn = qk * lax.rsqrt(ssq * (1.0 / HEAD_DIM) + NORM_EPS) * qkg_ref[...]
    reps = QK_COLS // LANES
    cos = jnp.concatenate([cos_ref[...]] * reps, axis=1)
    sin = jnp.concatenate([sin_ref[...]] * reps, axis=1)
    lane = lax.broadcasted_iota(I32, qkn.shape, 1)
    first = (lane & (2 * ROPE_PAIRS - 1)) < ROPE_PAIRS
    partner = jnp.where(first, pltpu.roll(qkn, QK_COLS - ROPE_PAIRS, 1), pltpu.roll(qkn, ROPE_PAIRS, 1))
    rot = qkn * cos + partner * sin
    q_ref[...] = (rot[:, :ATTN_WIDTH] * Q_SCALE).astype(BF16)
    kr = rot[:, ATTN_WIDTH:].astype(BF16)
    vt = p[:, RWKV_COLS + QK_COLS:].T
    row = lax.broadcasted_iota(I32, vt.shape, 0)
    for j in range(ATTN_KV_HEADS):
        k_ref[j] = kr[:, j * HEAD_DIM:(j + 1) * HEAD_DIM]
        vj = vt if j == 0 else pltpu.roll(vt, (ATTN_KV_HEADS - j) * HEAD_DIM, 0)
        vt_ref[j] = jnp.where(row < HEAD_DIM, vj, jnp.where(row == HEAD_DIM, 1.0, 0.0)).astype(BF16)


def _in_proj(x2, norm1, w_in, qk_gain, cos_t, sin_t, seg_qk, n, tm):
    t = x2.shape[0]
    nblk = n // tm
    return pl.pallas_call(
        _in_proj_kernel,
        grid=(t // tm,),
        in_specs=[
            pl.BlockSpec((tm, D_MODEL), lambda i: (i, 0)),
            _full((1, D_MODEL)),
            _full((D_MODEL, IN_COLS)),
            _full((1, QK_COLS)),
            pl.BlockSpec((tm, LANES), lambda i: (i % nblk, 0)),
            pl.BlockSpec((tm, LANES), lambda i: (i % nblk, 0)),
            _full((QK_COLS, QK_COLS)),
        ],
        out_specs=[
            pl.BlockSpec((tm, RWKV_COLS), lambda i: (i, 0)),
            pl.BlockSpec((tm, ATTN_WIDTH), lambda i: (i, 0)),
            pl.BlockSpec((ATTN_KV_HEADS, tm, HEAD_DIM), lambda i: (0, i, 0)),
            pl.BlockSpec((ATTN_KV_HEADS, LANES, tm), lambda i: (0, 0, i)),
        ],
        out_shape=[
            jax.ShapeDtypeStruct((t, RWKV_COLS), F32),
            jax.ShapeDtypeStruct((t, ATTN_WIDTH), BF16),
            jax.ShapeDtypeStruct((ATTN_KV_HEADS, t, HEAD_DIM), BF16),
            jax.ShapeDtypeStruct((ATTN_KV_HEADS, LANES, t), BF16),
        ],
        compiler_params=_cparams(("parallel",)),
        name="in_proj",
    )(x2, norm1, w_in, qk_gain, cos_t, sin_t, seg_qk)


def _attn_kernel(q_ref, k_ref, vt_ref, o_ref, qt_ref, m_ref, acc_ref, *, tq, tk, kb):
    ki = pl.program_id(2)
    nsub = KV_GROUP * tq // QSUB

    @pl.when(ki == 0)
    def _():
        qt = q_ref[...].astype(F32).T
        for j in range(ATTN_KV_HEADS):
            for g in range(KV_GROUP):
                h = j * KV_GROUP + g
                qt_ref[j, :, g * tq:(g + 1) * tq] = qt[h * HEAD_DIM:(h + 1) * HEAD_DIM, :].astype(BF16)
        m_ref[...] = jnp.full(m_ref.shape, -jnp.inf, F32)
        acc_ref[...] = jnp.zeros(acc_ref.shape, F32)

    tiles = [(j, s * QSUB, c * kb) for c in range(tk // kb) for j in range(ATTN_KV_HEADS) for s in range(nsub)]
    scores, probs, alphas = {}, {}, {}
    width = ATTN_PIPE_WIDTH
    ngroups = len(tiles) // width

    def group(gidx):
        return range(gidx * width, (gidx + 1) * width) if 0 <= gidx < ngroups else ()

    for step in range(ngroups + 2):
        for i in group(step):
            j, q0, k0 = tiles[i]
            scores[i] = _dot(k_ref[j, k0:k0 + kb, :], qt_ref[j, :, q0:q0 + QSUB])
        for i in group(step - 1):
            j, q0, k0 = tiles[i]
            st = scores.pop(i)
            m = m_ref[j, :, q0:q0 + QSUB]
            m_new = jnp.maximum(m, jnp.max(st, axis=0, keepdims=True))
            probs[i] = jnp.exp2(st - m_new).astype(BF16)
            alphas[i] = jnp.exp2(m - m_new)
            m_ref[j, :, q0:q0 + QSUB] = m_new
        for i in group(step - 2):
            j, q0, k0 = tiles[i]
            acc_ref[j, :, q0:q0 + QSUB] = (alphas.pop(i) * acc_ref[j, :, q0:q0 + QSUB]
                                           + _dot(vt_ref[j, 0:VROWS, k0:k0 + kb], probs.pop(i)))

    @pl.when(ki == pl.num_programs(2) - 1)
    def _():
        outs = []
        for j in range(ATTN_KV_HEADS):
            for g in range(KV_GROUP):
                a = acc_ref[j, :, g * tq:(g + 1) * tq]
                outs.append(a[:HEAD_DIM, :] / a[HEAD_DIM:HEAD_DIM + 1, :])
        o_ref[...] = jnp.concatenate(outs, axis=0).T.astype(o_ref.dtype)


def _attention(q, k, vt, b, n, tq, tk):
    t = b * n
    nq, nk = n // tq, n // tk
    nqry = KV_GROUP * tq
    kb = min(tk, KEY_CHUNK)
    return pl.pallas_call(
        functools.partial(_attn_kernel, tq=tq, tk=tk, kb=kb),
        grid=(b, nq, nk),
        in_specs=[
            pl.BlockSpec((tq, ATTN_WIDTH), lambda bi, qi, ki: (bi * nq + qi, 0)),
            pl.BlockSpec((ATTN_KV_HEADS, tk, HEAD_DIM), lambda bi, qi, ki: (0, bi * nk + ki, 0)),
            pl.BlockSpec((ATTN_KV_HEADS, LANES, tk), lambda bi, qi, ki: (0, 0, bi * nk + ki)),
        ],
        out_specs=pl.BlockSpec((tq, ATTN_WIDTH), lambda bi, qi, ki: (bi * nq + qi, 0)),
        out_shape=jax.ShapeDtypeStruct((t, ATTN_WIDTH), BF16),
        scratch_shapes=[
            pltpu.VMEM((ATTN_KV_HEADS, HEAD_DIM, nqry), BF16),
            pltpu.VMEM((ATTN_KV_HEADS, 1, nqry), F32),
            pltpu.VMEM((ATTN_KV_HEADS, VROWS, nqry), F32),
        ],
        compiler_params=_cparams(("parallel", "parallel", "arbitrary")),
        name="attention",
    )(q, k, vt)


def _shift_feat(p, prev_row, next_row, taps):
    c = p.shape[0]
    rows = lax.broadcasted_iota(I32, p.shape, 0)
    prev = jnp.where(rows == 0, prev_row, pltpu.roll(p, 1, 0))
    nxt = jnp.where(rows == c - 1, next_row, pltpu.roll(p, c - 1, 0))
    return taps[0:1] * prev + taps[1:2] * p + taps[2:3] * nxt


def _split_feat(feat):
    r = feat[:, :RWKV_WIDTH]
    k = feat[:, RWKV_WIDTH:2 * RWKV_WIDTH]
    v = feat[:, 2 * RWKV_WIDTH:3 * RWKV_WIDTH]
    o = 3 * RWKV_WIDTH
    wd = feat[:, o:o + W_LORA]
    ad = feat[:, o + W_LORA:o + W_LORA + A_LORA]
    gd = feat[:, o + W_LORA + A_LORA:]
    return r, k, v, wd, ad, gd


def _in_context_rate(ad_b, a0_row, a_up):
    return _sigmoid(a0_row + _dot(ad_b, a_up))


def _halo_rows(hp_ref, hn_ref, is_first, is_last):
    prev_row = jnp.where(is_first, 0.0, hp_ref[7:8, :])
    next_row = jnp.where(is_last, 0.0, hn_ref[0:1, :])
    return prev_row, next_row


def _masked_stack(x):
    lane = lax.broadcasted_iota(I32, x.shape, 1)
    return jnp.concatenate([jnp.where(lane < HEAD_DIM, x, 0.0), jnp.where(lane >= HEAD_DIM, x, 0.0)], axis=0)


def _wkv_chunks(ops, s_ref):
    c = CHUNK
    ti = lax.broadcasted_iota(I32, (c, c), 0)
    si = lax.broadcasted_iota(I32, (c, c), 1)
    rr = lax.broadcasted_iota(I32, (PAIR, PAIR), 0)
    cc = lax.broadcasted_iota(I32, (PAIR, PAIR), 1)
    eye = (rr == cc).astype(F32)
    rr = rr & (c - 1)
    cc = cc & (c - 1)

    chains = []
    for d, seq, (r, kd, v, an, bb, lw) in ops:
        rev = d == 1
        l_incl = ((si >= ti) if rev else (si <= ti)).astype(BF16)
        lw_hi, lw_mid, lw_lo = _split3(lw)
        cum = _dot(l_incl, lw_hi) + _dot(l_incl, lw_mid) + _dot(l_incl, lw_lo)
        tot = jnp.sum(lw, axis=0, keepdims=True)
        e_neg = jnp.exp(-cum)
        e_tot = jnp.exp(tot - cum)
        sc = dict(a=an * jnp.exp(cum - lw), r=r * jnp.exp(cum), b=bb * e_neg, k=kd * e_neg,
                  bh=bb * e_tot, kh=kd * e_tot, v=v)
        g = jnp.exp(tot)
        tri_incl = (cc >= rr) if rev else (cc <= rr)
        tri_strict = (cc > rr) if rev else (cc < rr)
        for p in range(N_PAIRS):
            sl = slice(p * PAIR, (p + 1) * PAIR)
            ch = dict(d=d, seq=seq, p=p, tri_incl=tri_incl, tri_strict=tri_strict, g=g[:, sl])
            for name in ("a", "b", "k", "bh", "kh", "v"):
                ch[name] = _masked_stack(sc[name][:, sl]).astype(BF16)
            ch["r"] = _masked_stack(sc["r"][:, sl])
            chains.append(ch)

    for ch in chains:
        gram = _dot_nt(jnp.concatenate([ch["a"], ch["r"].astype(BF16)], axis=0),
                       jnp.concatenate([ch["b"], ch["k"]], axis=0))
        ch["x"] = jnp.where(ch["tri_strict"], gram[:PAIR, :PAIR], 0.0)
        ch["m_ak"] = jnp.where(ch["tri_strict"], gram[:PAIR, PAIR:], 0.0).astype(BF16)
        ch["m_rb"] = jnp.where(ch["tri_incl"], gram[PAIR:, :PAIR], 0.0).astype(BF16)
        ch["m_rk"] = jnp.where(ch["tri_incl"], gram[PAIR:, PAIR:], 0.0).astype(BF16)
        ch["t_inv"] = eye

    nsteps = int(math.log2(c))
    for step in range(nsteps):
        last = step == nsteps - 1
        for ch in chains:
            yb = ch["x"].astype(BF16)
            tb = ch["t_inv"].astype(BF16)
            z = _dot(yb, tb if last else jnp.concatenate([yb, tb], axis=1))
            if last:
                ch["t_inv"] = ch["t_inv"] + z
            else:
                ch["x"] = z[:, :PAIR]
                ch["t_inv"] = ch["t_inv"] + z[:, PAIR:]

    for ch in chains:
        ch["mv"] = _dot(ch["m_ak"], ch["v"]).astype(BF16)
        ch["vk"] = _dot(ch["v"].astype(F32).T.astype(BF16), ch["kh"])
    for ch in chains:
        ch["ta"] = _dot(ch["t_inv"].astype(BF16), jnp.concatenate([ch["a"], ch["mv"]], axis=1))
    for ch in chains:
        ch["gh"] = _dot(ch["ta"].T.astype(BF16), ch["bh"])
        ch["mr"] = _dot(ch["m_rb"], ch["ta"].astype(BF16))
        ch["y_in"] = _dot(ch["m_rk"], ch["v"])
    state = {}
    parts = {}
    for ch in sorted(chains, key=lambda ch: ch["seq"]):
        key = (ch["d"], ch["p"])
        s_in = state[key] if key in state else s_ref[ch["d"], ch["p"]]
        s_b = s_in.astype(BF16)
        q_h = (ch["r"] + ch["mr"][:, :PAIR]).astype(BF16)
        y_ms = _dot_nt(q_h, s_b) + ch["mr"][:, PAIR:] + ch["y_in"]
        parts.setdefault((ch["d"], ch["seq"]), []).append(y_ms[:c] + y_ms[c:])
        gh = ch["gh"]
        state[key] = s_in * ch["g"] + _dot(s_b, gh[:PAIR].astype(BF16)) + gh[PAIR:] + ch["vk"]
    for (d, p), s_out in state.items():
        s_ref[d, p] = s_out
    return {key: jnp.concatenate(y, axis=1) for key, y in parts.items()}


def _wkv_kernel(pf_ref, hpf_ref, hnf_ref, pr_ref, hpr_ref, hnr_ref,
                taps_ref, w0_ref, wup_ref, a0_ref, aup_ref, kk_ref, ka_ref, seg_ref,
                yf_ref, yr_ref, s_ref, *, cps):
    c = pl.program_id(1)
    nc = pl.num_programs(1)

    @pl.when(c == 0)
    def _():
        s_ref[...] = jnp.zeros(s_ref.shape, F32)

    taps = taps_ref[...]
    seg = seg_ref[...]
    refs = ((pf_ref, hpf_ref, hnf_ref), (pr_ref, hpr_ref, hnr_ref))
    blocks = (c, nc - 1 - c)
    dirs = (0, 1)
    feats = []
    for d in dirs:
        p_ref, hp_ref, hn_ref = refs[d]
        prev_row, next_row = _halo_rows(hp_ref, hn_ref, blocks[d] == 0, blocks[d] == nc - 1)
        feats.append(_split_feat(_shift_feat(p_ref[...], prev_row, next_row, taps)))
    kks = [f[1] * kk_ref[...] for f in feats]
    norms = [_seg_sum(kk * kk, seg) for kk in kks]
    wlin = [_dot(jnp.tanh(feats[d][3]).astype(BF16), wup_ref[d]) for d in dirs]
    alin = [_dot(feats[d][4].astype(BF16), aup_ref[d]) for d in dirs]
    ops = []
    for d in dirs:
        r, k, v = feats[d][:3]
        kk = kks[d] * lax.rsqrt(jnp.maximum(norms[d], KK_EPS_SQ))
        lw = -DECAY_SCALE * _sigmoid(w0_ref[d:d + 1, :] + wlin[d])
        a = _sigmoid(a0_ref[d:d + 1, :] + alin[d])
        kd = k * (1.0 + (a - 1.0) * ka_ref[...])
        full = (r, kd, v, -kk, kk * a, lw)
        for seq in range(cps):
            ci = seq if d == 0 else cps - 1 - seq
            ops.append((d, seq, tuple(x[ci * CHUNK:(ci + 1) * CHUNK] for x in full)))
    ys = _wkv_chunks(ops, s_ref)
    for d, y_ref in enumerate((yf_ref, yr_ref)):
        for seq in range(cps):
            ci = seq if d == 0 else cps - 1 - seq
            y_ref[ci * CHUNK:(ci + 1) * CHUNK, :] = ys[(d, seq)]


def _wkv(feat, taps, w0, w_up, a0, a_up, k_k, k_a, seg_r, b, n):
    t = b * n
    cps = WKV_CHUNKS_PER_STEP
    rows = cps * CHUNK
    nc = n // rows
    c8 = rows // 8
    last8 = t // 8 - 1

    def fwd(bi, ci):
        return bi * nc + ci

    def rev(bi, ci):
        return bi * nc + (nc - 1 - ci)

    def specs(block_of):
        return [
            pl.BlockSpec((rows, RWKV_COLS), lambda bi, ci: (block_of(bi, ci), 0)),
            pl.BlockSpec((8, RWKV_COLS), lambda bi, ci: (jnp.maximum(block_of(bi, ci) * c8 - 1, 0), 0)),
            pl.BlockSpec((8, RWKV_COLS), lambda bi, ci: (jnp.minimum((block_of(bi, ci) + 1) * c8, last8), 0)),
        ]

    return pl.pallas_call(
        functools.partial(_wkv_kernel, cps=cps),
        grid=(b, nc),
        in_specs=specs(fwd) + specs(rev) + [
            _full((3, RWKV_COLS)),
            _full((2, RWKV_WIDTH)),
            _full((2, W_LORA, RWKV_WIDTH)),
            _full((2, RWKV_WIDTH)),
            _full((2, A_LORA, RWKV_WIDTH)),
            _full((1, RWKV_WIDTH)),
            _full((1, RWKV_WIDTH)),
            _full((RWKV_WIDTH, RWKV_WIDTH)),
        ],
        out_specs=[
            pl.BlockSpec((rows, RWKV_WIDTH), lambda bi, ci: (fwd(bi, ci), 0)),
            pl.BlockSpec((rows, RWKV_WIDTH), lambda bi, ci: (rev(bi, ci), 0)),
        ],
        out_shape=[jax.ShapeDtypeStruct((t, RWKV_WIDTH), F32)] * 2,
        scratch_shapes=[pltpu.VMEM((2, N_PAIRS, PAIR, PAIR), F32)],
        compiler_params=_cparams(("parallel", "arbitrary")),
        name="wkv",
    )(feat, feat, feat, feat, feat, feat, taps, w0, w_up, a0, a_up, k_k, k_a, seg_r)


def _rwkv_post_kernel(p_ref, hp_ref, hn_ref, yf_ref, yr_ref, taps_ref, a0_ref, aup_ref, ka_ref, rk_ref,
                      gup_ref, lng_ref, lnb_ref, seg_ref, o_ref, *, blocks_per_seq):
    i = pl.program_id(0)
    pos = i % blocks_per_seq
    prev_row, next_row = _halo_rows(hp_ref, hn_ref, pos == 0, pos == blocks_per_seq - 1)
    feat = _shift_feat(p_ref[...], prev_row, next_row, taps_ref[...])
    r, k, v, _, ad, gd = _split_feat(feat)
    seg = seg_ref[...]
    ad_b = ad.astype(BF16)
    a_sum = (_in_context_rate(ad_b, a0_ref[0:1, :], aup_ref[0])
             + _in_context_rate(ad_b, a0_ref[1:2, :], aup_ref[1]))
    k_bonus = k * (1.0 + (0.5 * a_sum - 1.0) * ka_ref[...])
    gate = _dot(_sigmoid(gd).astype(BF16), gup_ref[...])
    y = yf_ref[...] + yr_ref[...]
    mu = _seg_sum(y, seg) * (1.0 / HEAD_DIM)
    yc = y - mu
    var = _seg_sum(yc * yc, seg) * (1.0 / HEAD_DIM)
    yn = yc * lax.rsqrt(var + GN_EPS) * lng_ref[...] + lnb_ref[...]
    bonus = _seg_sum(r * k_bonus * rk_ref[...], seg) * v
    o_ref[...] = ((yn + bonus) * gate).astype(o_ref.dtype)


def _rwkv_post(feat, y_f, y_r, taps, a0, a_up, k_a, r_k, g_up, ln_g, ln_b, seg_r, n, tm):
    t = feat.shape[0]
    bps = n // tm
    t8 = tm // 8
    last8 = t // 8 - 1
    row = lambda i: (i, 0)
    return pl.pallas_call(
        functools.partial(_rwkv_post_kernel, blocks_per_seq=bps),
        grid=(t // tm,),
        in_specs=[
            pl.BlockSpec((tm, RWKV_COLS), row),
            pl.BlockSpec((8, RWKV_COLS), lambda i: (jnp.maximum(i * t8 - 1, 0), 0)),
            pl.BlockSpec((8, RWKV_COLS), lambda i: (jnp.minimum((i + 1) * t8, last8), 0)),
            pl.BlockSpec((tm, RWKV_WIDTH), row),
            pl.BlockSpec((tm, RWKV_WIDTH), row),
            _full((3, RWKV_COLS)),
            _full((2, RWKV_WIDTH)),
            _full((2, A_LORA, RWKV_WIDTH)),
            _full((1, RWKV_WIDTH)),
            _full((1, RWKV_WIDTH)),
            _full((G_LORA, RWKV_WIDTH)),
            _full((1, RWKV_WIDTH)),
            _full((1, RWKV_WIDTH)),
            _full((RWKV_WIDTH, RWKV_WIDTH)),
        ],
        out_specs=pl.BlockSpec((tm, RWKV_WIDTH), row),
        out_shape=jax.ShapeDtypeStruct((t, RWKV_WIDTH), BF16),
        compiler_params=_cparams(("parallel",)),
        name="rwkv_post",
    )(feat, feat, feat, y_f, y_r, taps, a0, a_up, k_a, r_k, g_up, ln_g, ln_b, seg_r)


def _out_proj_kernel(x_ref, ro_ref, ao_ref, wo_ref, g2_ref, wrh_ref, wrl_ref, h_ref, xn_ref, aff_ref):
    mix = _dot(ro_ref[...], wo_ref[:RWKV_WIDTH, :]) + _dot(ao_ref[...], wo_ref[RWKV_WIDTH:, :])
    h = x_ref[...] + mix
    h_ref[...] = h
    ms = jnp.mean(h * h, axis=-1, keepdims=True)
    xn = h * lax.rsqrt(ms + NORM_EPS) * g2_ref[...]
    xn_ref[...] = xn
    xh, xl = _split2(xn)
    wrh = wrh_ref[...]
    logits = _dot(xh, wrh) + _dot(xl, wrh) + _dot(xh, wrl_ref[...])
    lane = lax.broadcasted_iota(I32, logits.shape, 1)
    logits = jnp.where(lane < N_EXPERTS, logits, -jnp.inf)
    e = jnp.exp(logits - jnp.max(logits, axis=-1, keepdims=True))
    aff = e / jnp.sum(e, axis=-1, keepdims=True)
    aff_ref[...] = aff.T[:N_EXPERTS, :]


def _out_proj(x2, rwkv_out, attn_out, w_out, norm2, wr_hi, wr_lo, tm):
    t = x2.shape[0]
    row = lambda i: (i, 0)
    return pl.pallas_call(
        _out_proj_kernel,
        grid=(t // tm,),
        in_specs=[
            pl.BlockSpec((tm, D_MODEL), row),
            pl.BlockSpec((tm, RWKV_WIDTH), row),
            pl.BlockSpec((tm, ATTN_WIDTH), row),
            _full((D_MODEL, D_MODEL)),
            _full((1, D_MODEL)),
            _full((D_MODEL, LANES)),
            _full((D_MODEL, LANES)),
        ],
        out_specs=[
            pl.BlockSpec((tm, D_MODEL), row),
            pl.BlockSpec((tm, D_MODEL), row),
            pl.BlockSpec((N_EXPERTS, tm), lambda i: (0, i)),
        ],
        out_shape=[
            jax.ShapeDtypeStruct((t, D_MODEL), F32),
            jax.ShapeDtypeStruct((t, D_MODEL), F32),
            jax.ShapeDtypeStruct((N_EXPERTS, t), F32),
        ],
        compiler_params=_cparams(("parallel",)),
        name="out_proj",
    )(x2, rwkv_out, attn_out, w_out, norm2, wr_hi, wr_lo)


def _topk_kernel(aff_ref, inv_ref, rb_ref, *, cap, t, tb):
    nrows = t // LANES
    nblk = t // tb
    rows_per_blk = tb // LANES

    def bits(sl):
        return pltpu.bitcast(aff_ref[:, sl], I32)

    def bisect(i, prefix):
        cand = prefix | jnp.left_shift(jnp.int32(1), 30 - i)
        cnt = jnp.sum((bits(slice(None)) >= cand).astype(I32), axis=1, keepdims=True)
        return jnp.where(cnt >= cap, cand, prefix)

    tau = lax.fori_loop(0, 31, bisect, jnp.zeros((N_EXPERTS, 1), I32))
    n_gt = jnp.sum((bits(slice(None)) > tau).astype(I32), axis=1, keepdims=True)
    need = (cap - n_gt).astype(F32)

    upper = (lax.broadcasted_iota(I32, (tb, tb), 0) < lax.broadcasted_iota(I32, (tb, tb), 1)).astype(BF16)
    tok_row = lax.broadcasted_iota(I32, (tb, nrows), 0) >> 7
    out_row = lax.broadcasted_iota(I32, (tb, nrows), 1)

    def block(b, carry):
        c_eq, c_pos, row_cnt = carry
        start = pl.multiple_of(b * tb, tb)
        xs = bits(pl.ds(start, tb))
        gt = xs > tau
        eq = xs == tau
        eq_b = eq.astype(BF16)
        rank_eq = c_eq + _dot(eq_b, upper)
        sel = gt | (eq & (rank_eq < need))
        sel_b = sel.astype(BF16)
        pos = c_pos + _dot(sel_b, upper)
        inv_ref[:, pl.ds(start, tb)] = jnp.where(sel, pos.astype(I32), -1)
        place = (out_row == tok_row + b * rows_per_blk).astype(BF16)
        row_cnt = row_cnt + _dot(sel_b, place)
        c_eq = c_eq + jnp.sum(eq_b.astype(F32), axis=1, keepdims=True)
        c_pos = c_pos + jnp.sum(sel_b.astype(F32), axis=1, keepdims=True)
        return c_eq, c_pos, row_cnt

    zero = jnp.zeros((N_EXPERTS, 1), F32)
    _, _, row_cnt = lax.fori_loop(0, nblk, block, (zero, zero, jnp.zeros((N_EXPERTS, nrows), F32)))
    upper_r = (lax.broadcasted_iota(I32, (nrows, nrows), 0) < lax.broadcasted_iota(I32, (nrows, nrows), 1)).astype(BF16)
    rb_ref[...] = _dot(row_cnt.astype(BF16), upper_r).astype(I32)


def _topk(aff_t, cap):
    t = aff_t.shape[1]
    tb = min(512, t)
    return pl.pallas_call(
        functools.partial(_topk_kernel, cap=cap, t=t, tb=tb),
        in_specs=[_full((N_EXPERTS, t))],
        out_specs=[_full((N_EXPERTS, t)), _full((N_EXPERTS, t // LANES))],
        out_shape=[jax.ShapeDtypeStruct((N_EXPERTS, t), I32), jax.ShapeDtypeStruct((N_EXPERTS, t // LANES), I32)],
        compiler_params=pltpu.CompilerParams(vmem_limit_bytes=VMEM_LIMIT),
        name="topk",
    )(aff_t)


def _compact_kernel(rb_ref, inv_ref, idx_ref, acc_ref, *, nrows, ntiles):
    e = pl.program_id(0)
    acc_ref[...] = jnp.zeros(acc_ref.shape, F32)
    sub = lax.broadcasted_iota(I32, (8, LANES), 0)
    lane_tok = lax.broadcasted_iota(I32, (8, LANES), 1)
    slot_off = lax.broadcasted_iota(I32, (LANES, LANES), 0)

    def body(r, _):
        start = pl.multiple_of(r * LANES, LANES)
        inv_row = inv_ref[0, :, pl.ds(start, LANES)]
        tok = lane_tok + r * LANES
        lmat = jnp.where(sub == 0, (tok >> 8).astype(F32),
                         jnp.where(sub == 1, (tok & 255).astype(F32), 0.0)).astype(BF16)
        q0 = rb_ref[e * nrows + r] >> 7
        inv_b = jnp.broadcast_to(inv_row, (LANES, LANES))
        for dq in range(2):
            onehot_t = (inv_b == slot_off + (q0 + dq) * LANES).astype(BF16)
            acc_ref[q0 + dq] += _dot_nt(lmat, onehot_t)
        return 0

    lax.fori_loop(0, nrows, body, 0, unroll=8)
    idx_ref[0] = (acc_ref[0:ntiles, 0, :] * 256.0 + acc_ref[0:ntiles, 1, :]).astype(I32)


def _compact(rb_flat, inv, cap):
    t = inv.shape[1]
    nrows = t // LANES
    ntiles = cap // LANES
    return pl.pallas_call(
        functools.partial(_compact_kernel, nrows=nrows, ntiles=ntiles),
        grid_spec=pltpu.PrefetchScalarGridSpec(
            num_scalar_prefetch=1,
            grid=(N_EXPERTS,),
            in_specs=[pl.BlockSpec((1, 1, t), lambda e, rb: (e, 0, 0))],
            out_specs=pl.BlockSpec((1, ntiles, LANES), lambda e, rb: (e, 0, 0)),
            scratch_shapes=[pltpu.VMEM((ntiles + 2, 8, LANES), F32)],
        ),
        out_shape=jax.ShapeDtypeStruct((N_EXPERTS, ntiles, LANES), I32),
        compiler_params=_cparams(("arbitrary",)),
        name="compact",
    )(rb_flat, inv.reshape(N_EXPERTS, 1, t))


def _expert_kernel(idx_ref, idxn_ref, x_hbm, wg_ref, wu_ref, wd_ref, o_ref, xbuf, sems, *, ts, nsteps):
    g = pl.program_id(0) * pl.num_programs(1) + pl.program_id(1)
    slot = g % 2

    def row_copy(ids, j, sl):
        return pltpu.make_async_copy(x_hbm.at[pl.ds(ids[0, 0, j], 1)], xbuf.at[sl, pl.ds(j, 1)], sems.at[sl])

    @pl.when(g == 0)
    def _():
        for j in range(ts):
            row_copy(idx_ref, j, 0).start()

    def body(slot):
        for j in range(ts):
            row_copy(idx_ref, j, slot).wait()
        quarter = ts // 4
        for j in range(0, quarter):
            row_copy(idxn_ref, j, 1 - slot).start()
        xs = xbuf[slot].astype(BF16)
        gp = _dot(xs, wg_ref[0])
        for j in range(quarter, 2 * quarter):
            row_copy(idxn_ref, j, 1 - slot).start()
        up = _dot(xs, wu_ref[0])
        for j in range(2 * quarter, 3 * quarter):
            row_copy(idxn_ref, j, 1 - slot).start()
        hid = (gp * _sigmoid(gp) * up).astype(BF16)
        o_ref[...] = _dot(hid, wd_ref[0]).astype(o_ref.dtype)
        for j in range(3 * quarter, ts):
            row_copy(idxn_ref, j, 1 - slot).start()

    @pl.when(slot == 0)
    def _():
        body(0)

    @pl.when(slot == 1)
    def _():
        body(1)

    @pl.when(g == nsteps - 1)
    def _():
        for j in range(ts):
            row_copy(idxn_ref, j, 1 - slot).wait()


def _experts(idx, xn, w_gate, w_up, w_down, cap, ts):
    nt = cap // ts
    nsteps = N_EXPERTS * nt
    idx3 = idx.reshape(nsteps, 1, ts)
    wspec = lambda a, b: pl.BlockSpec((1, a, b), lambda e, i: (e, 0, 0))
    return pl.pallas_call(
        functools.partial(_expert_kernel, ts=ts, nsteps=nsteps),
        grid=(N_EXPERTS, nt),
        in_specs=[
            pl.BlockSpec((1, 1, ts), lambda e, i: (e * nt + i, 0, 0), memory_space=pltpu.SMEM),
            pl.BlockSpec((1, 1, ts), lambda e, i: (jnp.minimum(e * nt + i + 1, nsteps - 1), 0, 0),
                         memory_space=pltpu.SMEM),
            pl.BlockSpec(memory_space=pl.ANY),
            wspec(D_MODEL, D_EXPERT),
            wspec(D_MODEL, D_EXPERT),
            wspec(D_EXPERT, D_MODEL),
        ],
        out_specs=pl.BlockSpec((ts, D_MODEL), lambda e, i: (e * nt + i, 0)),
        out_shape=jax.ShapeDtypeStruct((N_EXPERTS * cap, D_MODEL), BF16),
        scratch_shapes=[
            pltpu.VMEM((2, ts, D_MODEL), F32),
            pltpu.SemaphoreType.DMA((2,)),
        ],
        compiler_params=_cparams(("arbitrary", "arbitrary")),
        name="experts",
    )(idx3, idx3, xn, w_gate, w_up, w_down)


def _combine_kernel(rb_ref, h_ref, inv_ref, aff_ref, o_hbm, y_ref, cbuf, sems, *, nrows, cap):
    r = pl.program_id(0)
    slot = r % 2

    def chunk_copy(rr, e, sl):
        p0 = jnp.minimum(rb_ref[e * nrows + rr] & ~(CHUNK_ALIGN - 1), cap - CHUNK_ROWS)
        return pltpu.make_async_copy(o_hbm.at[e, pl.ds(pl.multiple_of(p0, CHUNK_ALIGN), CHUNK_ROWS), :],
                                     cbuf.at[sl, e], sems.at[sl]), p0

    @pl.when(r == 0)
    def _():
        for e in range(N_EXPERTS):
            chunk_copy(0, e, 0)[0].start()

    nxt = jnp.minimum(r + 1, nrows - 1)
    for e in range(N_EXPERTS):
        chunk_copy(nxt, e, 1 - slot)[0].start()

    pad = jnp.zeros((LANES - N_EXPERTS, LANES), F32)
    inv_t = jnp.concatenate([inv_ref[...].astype(F32), pad], axis=0).T
    aff_t = jnp.concatenate([aff_ref[...], pad], axis=0).T
    lane = lax.broadcasted_iota(I32, (LANES, CHUNK_ROWS), 1)
    acc = h_ref[...]
    bases = []
    for e in range(N_EXPERTS):
        cp, p0 = chunk_copy(r, e, slot)
        cp.wait()
        bases.append(p0)
    for e in range(N_EXPERTS):
        onehot = inv_t[:, e:e + 1] == (lane + bases[e]).astype(F32)
        gsel = jnp.where(onehot, aff_t[:, e:e + 1], 0.0).astype(BF16)
        acc = acc + _dot(gsel, cbuf[slot, e])
    y_ref[...] = acc

    @pl.when(r == nrows - 1)
    def _():
        for e in range(N_EXPERTS):
            chunk_copy(nxt, e, 1 - slot)[0].wait()


def _combine(rb_flat, h, inv, aff_t, eout, cap):
    t = h.shape[0]
    nrows = t // LANES
    return pl.pallas_call(
        functools.partial(_combine_kernel, nrows=nrows, cap=cap),
        grid_spec=pltpu.PrefetchScalarGridSpec(
            num_scalar_prefetch=1,
            grid=(nrows,),
            in_specs=[
                pl.BlockSpec((LANES, D_MODEL), lambda r, rb: (r, 0)),
                pl.BlockSpec((N_EXPERTS, LANES), lambda r, rb: (0, r)),
                pl.BlockSpec((N_EXPERTS, LANES), lambda r, rb: (0, r)),
                pl.BlockSpec(memory_space=pl.ANY),
            ],
            out_specs=pl.BlockSpec((LANES, D_MODEL), lambda r, rb: (r, 0)),
            scratch_shapes=[
                pltpu.VMEM((2, N_EXPERTS, CHUNK_ROWS, D_MODEL), BF16),
                pltpu.SemaphoreType.DMA((2,)),
            ],
        ),
        out_shape=jax.ShapeDtypeStruct((t, D_MODEL), F32),
        compiler_params=_cparams(("arbitrary",)),
        name="combine",
    )(rb_flat, h, inv, aff_t, eout.reshape(N_EXPERTS, cap, D_MODEL))


def _rope_tables(n):
    pos = jnp.arange(n, dtype=F32)
    row = jnp.floor(pos / GRID_W)
    col = pos - row * GRID_W
    inv_freq = ROPE_THETA ** (-jnp.arange(ROPE_PAIRS, dtype=F32) / ROPE_PAIRS)
    ang_r = row[:, None] * inv_freq
    ang_c = col[:, None] * inv_freq
    cos = jnp.concatenate([jnp.cos(ang_r), jnp.cos(ang_r), jnp.cos(ang_c), jnp.cos(ang_c)], axis=1)
    sin = jnp.concatenate([-jnp.sin(ang_r), jnp.sin(ang_r), -jnp.sin(ang_c), jnp.sin(ang_c)], axis=1)
    reps = LANES // HEAD_DIM
    return jnp.tile(cos, (1, reps)), jnp.tile(sin, (1, reps))


def _block_diag_ones(width):
    i = jnp.arange(width) // HEAD_DIM
    return (i[:, None] == i[None, :]).astype(BF16)


def _pick(n, prefs):
    for p in prefs:
        if n % p == 0:
            return p
    raise ValueError(f"no block size in {prefs} divides {n}")


def _layer(x, prm):
    b, n, _ = x.shape
    t = b * n
    cap = CAPACITY_FACTOR * t // N_EXPERTS
    assert n % (WKV_CHUNKS_PER_STEP * CHUNK) == 0 and n % GRID_W == 0 and t % LANES == 0 and cap % LANES == 0
    x2 = x.reshape(t, D_MODEL)
    cos_t, sin_t = _rope_tables(n)

    tm = _pick(n, (512, 256, 128, 64))
    feat, q, k, vt = _in_proj(x2, prm["norm1"], prm["w_in"], prm["qk_gain"], cos_t, sin_t, prm["seg_qk"], n, tm)

    tq = _pick(n, (1024, 512, 256, 128))
    tk = _pick(n, (2048, 1024, 512, 256, 128))
    attn_out = _attention(q, k, vt, b, n, tq, tk)

    y_f, y_r = _wkv(feat, prm["taps"], prm["w0"], prm["w_up"], prm["a0"], prm["a_up"], prm["k_k"], prm["k_a"],
                    prm["seg_r"], b, n)
    tp = _pick(n, (256, 128, 64))
    rwkv_out = _rwkv_post(feat, y_f, y_r, prm["taps"], prm["a0"], prm["a_up"], prm["k_a"], prm["r_k"],
                          prm["g_up"], prm["ln_g"], prm["ln_b"], prm["seg_r"], n, tp)

    h, xn, aff_t = _out_proj(x2, rwkv_out, attn_out, prm["w_out"], prm["norm2"], prm["wr_hi"], prm["wr_lo"], tm)

    inv, rb = _topk(aff_t, cap)
    rb_flat = rb.reshape(-1)
    idx = _compact(rb_flat, inv, cap)
    ts = _pick(cap, (256, 128))
    eout = _experts(idx.reshape(N_EXPERTS, cap), xn, prm["w_gate"], prm["w_up_e"], prm["w_down"], cap, ts)
    y = _combine(rb_flat, h, inv, aff_t, eout, cap)
    return y.reshape(b, n, D_MODEL)


def _prepare(norm1, w_in, shift_taps, w0, w_lora_up, a0, a_lora_up, k_k, k_a, r_k, g_lora_up, ln_g, ln_b,
             q_norm, k_norm, w_out, norm2, w_router, expert_w_gate, expert_w_up, expert_w_down):
    l = 0
    wr = jnp.pad(w_router[l], ((0, 0), (0, LANES - N_EXPERTS)))
    wr_hi = wr.astype(BF16)
    wr_lo = (wr - wr_hi.astype(F32)).astype(BF16)
    qk_gain = jnp.concatenate([jnp.tile(q_norm[l], ATTN_Q_HEADS), jnp.tile(k_norm[l], ATTN_KV_HEADS)])
    return dict(
        norm1=norm1[l].reshape(1, D_MODEL),
        w_in=w_in[l].astype(BF16),
        taps=shift_taps[l],
        w0=w0[l],
        w_up=w_lora_up[l].astype(BF16),
        a0=a0[l],
        a_up=a_lora_up[l].astype(BF16),
        k_k=k_k[l].reshape(1, RWKV_WIDTH),
        k_a=k_a[l].reshape(1, RWKV_WIDTH),
        r_k=r_k[l].reshape(1, RWKV_WIDTH),
        g_up=g_lora_up[l].astype(BF16),
        ln_g=ln_g[l].reshape(1, RWKV_WIDTH),
        ln_b=ln_b[l].reshape(1, RWKV_WIDTH),
        qk_gain=qk_gain.reshape(1, QK_COLS),
        w_out=w_out[l].astype(BF16),
        norm2=norm2[l].reshape(1, D_MODEL),
        wr_hi=wr_hi,
        wr_lo=wr_lo,
        w_gate=expert_w_gate[l].astype(BF16),
        w_up_e=expert_w_up[l].astype(BF16),
        w_down=expert_w_down[l].astype(BF16),
        seg_qk=_block_diag_ones(QK_COLS),
        seg_r=_block_diag_ones(RWKV_WIDTH),
    )


def kernel(x_prompt, x_sample, norm1, w_in, shift_taps, w0, w_lora_up, a0, a_lora_up, k_k, k_a, r_k, g_lora_up,
           ln_g, ln_b, q_norm, k_norm, w_out, norm2, w_router, expert_w_gate, expert_w_up, expert_w_down):
    assert norm1.shape[0] == 1, "single-layer trunk"
    prm = _prepare(norm1, w_in, shift_taps, w0, w_lora_up, a0, a_lora_up, k_k, k_a, r_k, g_lora_up, ln_g, ln_b,
                   q_norm, k_norm, w_out, norm2, w_router, expert_w_gate, expert_w_up, expert_w_down)
    return (_layer(x_prompt, prm), _layer(x_sample, prm))
```
